```python
import jax
import jax.numpy as jnp
from jax import lax
import numpy as np

D_MODEL = 1024
BATCH = 8
SEQ = 2048
DEPTH = 2

GRID_W = 64
CTX_LEN = 256
CONV_DIM = 512
CONV_WIDTH = 31
N_HEADS = 8
N_KV_HEADS = 2
HEAD_DIM = 64
ATTN_DIM = N_HEADS * HEAD_DIM
KV_DIM = N_KV_HEADS * HEAD_DIM
WINDOW = 128
BLOCK = 128
ROPE_BASE = 10000.0
SGU_DIM = 512
SGU_GROUPS = 8
SGU_CHUNK = 128
N_BRANCH = 3
FFN_DIM = 2816
FFN_CONV_WIDTH = 3
EPS = 1e-6
NEG_INF = -1e30
IN_SPLITS = (CONV_DIM, CONV_DIM, ATTN_DIM, KV_DIM, KV_DIM, SGU_DIM, SGU_DIM, N_BRANCH * D_MODEL)
IN_DIM = 2 * CONV_DIM + ATTN_DIM + 2 * KV_DIM + 2 * SGU_DIM + N_BRANCH * D_MODEL
KV_START = 2 * CONV_DIM + ATTN_DIM

kernel_name = 'hybrid_conv_swa_gmlp_dit_block'


def rms_norm(x, g):
    xf = x.astype(jnp.float32)
    y = xf * lax.rsqrt(jnp.mean(xf * xf, axis=-1, keepdims=True) + EPS)
    return (y * g.astype(jnp.float32)).astype(x.dtype)


def layer_norm(x, g, b):
    xf = x.astype(jnp.float32)
    mu = jnp.mean(xf, axis=-1, keepdims=True)
    var = jnp.mean(jnp.square(xf - mu), axis=-1, keepdims=True)
    y = (xf - mu) * lax.rsqrt(var + EPS)
    return (y * g.astype(jnp.float32) + b.astype(jnp.float32)).astype(x.dtype)


def modulation(cond, w, b):
    m = jax.nn.silu(cond) @ w + b
    m = m.reshape(-1, 1, 6, D_MODEL)
    return [m[:, :, i] for i in range(6)]


def modulate(h, shift, scale):
    return h * (1.0 + scale) + shift


def dwconv(x, w, b):
    k = w.shape[0]
    y = lax.conv_general_dilated(
        x, w[:, None, :], window_strides=(1,), padding=[(k // 2, k // 2)],
        dimension_numbers=('NWC', 'WIO', 'NWC'), feature_group_count=x.shape[-1])
    return y + b


def rope_axis(x, pos):
    half = x.shape[-1] // 2
    inv = jnp.power(ROPE_BASE, -jnp.arange(half, dtype=jnp.float32) / half)
    ang = pos.astype(jnp.float32)[:, None] * inv[None, :]
    cos = jnp.cos(ang)[:, None, :]
    sin = jnp.sin(ang)[:, None, :]
    xf = x.astype(jnp.float32)
    x1, x2 = xf[..., :half], xf[..., half:]
    return jnp.concatenate([x1 * cos - x2 * sin, x2 * cos + x1 * sin], axis=-1).astype(x.dtype)


def rope_2d(x, rows, cols):
    n = x.shape[-1] // 2
    return jnp.concatenate([rope_axis(x[..., :n], rows), rope_axis(x[..., n:], cols)], axis=-1)


def split_in(z):
    bounds = np.cumsum(IN_SPLITS)[:-1].tolist()
    return jnp.split(z, bounds, axis=-1)


def qk_heads(t, n_heads, g):
    return rms_norm(t.reshape(*t.shape[:-1], n_heads, HEAD_DIM), g)


def window_attention(q, k, v, kc, vc, sink):
    bsz, s = q.shape[0], q.shape[1]
    nb = s // BLOCK
    grp = N_HEADS // N_KV_HEADS
    scale = HEAD_DIM ** -0.5
    qb = q.reshape(bsz, nb, BLOCK, N_KV_HEADS, grp, HEAD_DIM)
    pad = ((0, 0), (BLOCK, BLOCK), (0, 0), (0, 0))
    kp = jnp.pad(k, pad).reshape(bsz, nb + 2, BLOCK, N_KV_HEADS, HEAD_DIM)
    vp = jnp.pad(v, pad).reshape(bsz, nb + 2, BLOCK, N_KV_HEADS, HEAD_DIM)
    kb = jnp.concatenate([kp[:, :-2], kp[:, 1:-1], kp[:, 2:]], axis=2)
    vb = jnp.concatenate([vp[:, :-2], vp[:, 1:-1], vp[:, 2:]], axis=2)
    s_loc = jnp.einsum('bnqhgd,bnkhd->bnhgqk', qb, kb).astype(jnp.float32) * scale
    s_ctx = jnp.einsum('bnqhgd,bchd->bnhgqc', qb, kc).astype(jnp.float32) * scale
    r = jnp.arange(BLOCK)[:, None]
    j = jnp.arange(3 * BLOCK)[None, :]
    in_window = jnp.abs(j - BLOCK - r) <= WINDOW
    kpos = (jnp.arange(nb)[:, None] - 1) * BLOCK + jnp.arange(3 * BLOCK)[None, :]
    in_range = (kpos >= 0) & (kpos < s)
    mask = in_window[None] & in_range[:, None, :]
    s_loc = jnp.where(mask[None, :, None, None], s_loc, NEG_INF)
    sink_l = jnp.broadcast_to(
        sink.astype(jnp.float32).reshape(N_KV_HEADS, grp)[None, None, :, :, None, None],
        s_loc.shape[:-1] + (1,))
    p = jax.nn.softmax(jnp.concatenate([s_loc, s_ctx, sink_l], axis=-1), axis=-1)
    nl = 3 * BLOCK
    nc = kc.shape[1]
    p_loc = p[..., :nl].astype(v.dtype)
    p_ctx = p[..., nl:nl + nc].astype(v.dtype)
    o = (jnp.einsum('bnhgqk,bnkhd->bnqhgd', p_loc, vb)
         + jnp.einsum('bnhgqc,bchd->bnqhgd', p_ctx, vc))
    return o.reshape(bsz, s, ATTN_DIM)


def context_attention(q, k, v, sink):
    bsz, n = q.shape[0], q.shape[1]
    grp = N_HEADS // N_KV_HEADS
    qg = q.reshape(bsz, n, N_KV_HEADS, grp, HEAD_DIM)
    s = jnp.einsum('bqhgd,bkhd->bhgqk', qg, k).astype(jnp.float32) * (HEAD_DIM ** -0.5)
    sink_l = jnp.broadcast_to(
        sink.astype(jnp.float32).reshape(N_KV_HEADS, grp)[None, :, :, None, None],
        s.shape[:-1] + (1,))
    p = jax.nn.softmax(jnp.concatenate([s, sink_l], axis=-1), axis=-1)[..., :n]
    o = jnp.einsum('bhgqk,bkhd->bqhgd', p.astype(v.dtype), v)
    return o.reshape(bsz, n, ATTN_DIM)


def conv_module(a, b, p):
    y = a * jax.nn.sigmoid(b)
    y = dwconv(y, p['conv_dw_w'], p['conv_dw_b'])
    y = jax.nn.silu(layer_norm(y, p['conv_ln_g'], p['conv_ln_b']))
    return y @ p['conv_out']


def sgu_module(u, v, p):
    u = jax.nn.gelu(u)
    v = layer_norm(jax.nn.gelu(v), p['sgu_ln_g'], p['sgu_ln_b'])
    bsz, n = v.shape[0], v.shape[1]
    vc = v.reshape(bsz, n // SGU_CHUNK, SGU_CHUNK, SGU_GROUPS, SGU_DIM // SGU_GROUPS)
    mixed = jnp.einsum('gpq,bnqgc->bnpgc', p['sgu_w'], vc) + p['sgu_b'].T[:, :, None]
    y = u * mixed.reshape(bsz, n, SGU_DIM)
    return y @ p['sgu_out']


def mixer_merge(z, attn, p):
    a_conv, b_conv, _, _, _, u, v, g = z
    y_conv = conv_module(a_conv, b_conv, p)
    y_attn = attn @ p['attn_out']
    y_sgu = sgu_module(u, v, p)
    gates = jax.nn.sigmoid(g.reshape(*g.shape[:-1], N_BRANCH, D_MODEL) + p['gate_b'])
    m = gates[..., 0, :] * y_conv + gates[..., 1, :] * y_attn + gates[..., 2, :] * y_sgu
    return m @ p['w_o']


def conv_ffn(h, p):
    z = h @ p['ffn_up']
    z = dwconv(z, p['ffn_dw_w'], p['ffn_dw_b'])
    a, b = jnp.split(z, 2, axis=-1)
    return (jax.nn.silu(a) * b) @ p['ffn_down']


def setup_inputs(seed: int = 0) -> dict:
    key = jax.random.key(seed)
    ks = jax.random.split(key, 32)

    def nrm(i, shape, scale):
        return jax.random.normal(ks[i], shape, jnp.float32) * scale

    L = DEPTH
    D = D_MODEL
    return {
        'x': nrm(0, (BATCH, SEQ, D), 1.0),
        'c': nrm(1, (BATCH, D), 1.0),
        'ctx': nrm(2, (BATCH, CTX_LEN, D), 1.0),
        'c_ctx': nrm(3, (D,), 1.0),
        'ada_w': nrm(4, (L, D, 6 * D), 0.5 * D ** -0.5),
        'ada_b': nrm(5, (L, 6 * D), 0.01),
        'norm1_g': 1.0 + nrm(6, (L, D), 0.05),
        'norm2_g': 1.0 + nrm(7, (L, D), 0.05),
        'w_in': nrm(8, (L, D, IN_DIM), D ** -0.5),
        'gate_b': nrm(9, (L, N_BRANCH, D), 0.01),
        'conv_dw_w': nrm(10, (L, CONV_WIDTH, CONV_DIM), CONV_WIDTH ** -0.5),
        'conv_dw_b': nrm(11, (L, CONV_DIM), 0.01),
        'conv_ln_g': 1.0 + nrm(12, (L, CONV_DIM), 0.05),
        'conv_ln_b': nrm(13, (L, CONV_DIM), 0.01),
        'conv_out': nrm(14, (L, CONV_DIM, D), CONV_DIM ** -0.5),
        'q_norm_g': 1.0 + nrm(15, (L, HEAD_DIM), 0.05),
        'k_norm_g': 1.0 + nrm(16, (L, HEAD_DIM), 0.05),
        'attn_sink': nrm(17, (L, N_HEADS), 0.5),
        'attn_out': nrm(18, (L, ATTN_DIM, D), ATTN_DIM ** -0.5),
        'sgu_ln_g': 1.0 + nrm(19, (L, SGU_DIM), 0.05),
        'sgu_ln_b': nrm(20, (L, SGU_DIM), 0.01),
        'sgu_w': nrm(21, (L, SGU_GROUPS, SGU_CHUNK, SGU_CHUNK), SGU_CHUNK ** -0.5),
        'sgu_b': nrm(22, (L, SGU_GROUPS, SGU_CHUNK), 0.01),
        'sgu_out': nrm(23, (L, SGU_DIM, D), SGU_DIM ** -0.5),
        'w_o': nrm(24, (L, D, D), D ** -0.5),
        'ffn_up': nrm(25, (L, D, 2 * FFN_DIM), D ** -0.5),
        'ffn_dw_w': nrm(26, (L, FFN_CONV_WIDTH, 2 * FFN_DIM), FFN_CONV_WIDTH ** -0.5),
        'ffn_dw_b': nrm(27, (L, 2 * FFN_DIM), 0.01),
        'ffn_down': nrm(28, (L, FFN_DIM, D), FFN_DIM ** -0.5),
    }


def reference(x, c, ctx, c_ctx, ada_w, ada_b, norm1_g, norm2_g, w_in, gate_b,
              conv_dw_w, conv_dw_b, conv_ln_g, conv_ln_b, conv_out,
              q_norm_g, k_norm_g, attn_sink, attn_out,
              sgu_ln_g, sgu_ln_b, sgu_w, sgu_b, sgu_out, w_o,
              ffn_up, ffn_dw_w, ffn_dw_b, ffn_down):
    s = x.shape[1]
    rows_n = s // GRID_W
    rows = jnp.repeat(jnp.arange(rows_n), GRID_W)
    cols = jnp.tile(jnp.arange(GRID_W), rows_n)
    for l in range(DEPTH):
        last = l == DEPTH - 1
        p = {
            'gate_b': gate_b[l], 'w_o': w_o[l],
            'conv_dw_w': conv_dw_w[l], 'conv_dw_b': conv_dw_b[l],
            'conv_ln_g': conv_ln_g[l], 'conv_ln_b': conv_ln_b[l], 'conv_out': conv_out[l],
            'attn_out': attn_out[l],
            'sgu_ln_g': sgu_ln_g[l], 'sgu_ln_b': sgu_ln_b[l], 'sgu_w': sgu_w[l],
            'sgu_b': sgu_b[l], 'sgu_out': sgu_out[l],
            'ffn_up': ffn_up[l], 'ffn_dw_w': ffn_dw_w[l], 'ffn_dw_b': ffn_dw_b[l],
            'ffn_down': ffn_down[l],
        }
        sx1, cx1, gx1, sx2, cx2, gx2 = modulation(c, ada_w[l], ada_b[l])
        sc1, cc1, gc1, sc2, cc2, gc2 = modulation(c_ctx[None, :], ada_w[l], ada_b[l])

        hx = modulate(rms_norm(x, norm1_g[l]), sx1, cx1)
        hc = modulate(rms_norm(ctx, norm1_g[l]), sc1, cc1)

        if last:
            kv_c = hc @ w_in[l][:, KV_START:KV_START + 2 * KV_DIM]
            kc_flat, vc_flat = jnp.split(kv_c, 2, axis=-1)
        else:
            zc = split_in(hc @ w_in[l])
            kc_flat, vc_flat = zc[3], zc[4]
        kc = qk_heads(kc_flat, N_KV_HEADS, k_norm_g[l])
        vc = vc_flat.reshape(*vc_flat.shape[:-1], N_KV_HEADS, HEAD_DIM)

        zx = split_in(hx @ w_in[l])
        qx = rope_2d(qk_heads(zx[2], N_HEADS, q_norm_g[l]), rows, cols)
        kx = rope_2d(qk_heads(zx[3], N_KV_HEADS, k_norm_g[l]), rows, cols)
        vx = zx[4].reshape(*zx[4].shape[:-1], N_KV_HEADS, HEAD_DIM)
        ax = window_attention(qx, kx, vx, kc, vc, attn_sink[l])
        x_new = x + gx1 * mixer_merge(zx, ax, p)
        x_new = x_new + gx2 * conv_ffn(modulate(rms_norm(x_new, norm2_g[l]), sx2, cx2), p)

        if not last:
            qc = qk_heads(zc[2], N_HEADS, q_norm_g[l])
            ac = context_attention(qc, kc, vc, attn_sink[l])
            ctx = ctx + gc1 * mixer_merge(zc, ac, p)
            ctx = ctx + gc2 * conv_ffn(modulate(rms_norm(ctx, norm2_g[l]), sc2, cc2), p)
        x = x_new
    return x
```

```python
import functools

import jax
import jax.numpy as jnp
from jax import lax
from jax.experimental import pallas as pl
from jax.experimental.pallas import tpu as pltpu

F32 = jnp.float32
BF16 = jnp.bfloat16

EPS = 1e-6
NEG_INF = -1e30
ROPE_BASE = 10000.0
GRID_W = 64
HEAD_DIM = 64
N_HEADS = 8
N_KV_HEADS = 2
WINDOW = 128
CONV_DIM = 512
CONV_WIDTH = 31
SGU_DIM = 512
SGU_GROUPS = 8
SGU_CHUNK = 128
FFN_CONV_WIDTH = 3
N_BRANCH = 3

LANES = 128
CONV_HALO = 16
FFN_HALO = 8
FFN_CHUNK = 256
VMEM_LIMIT = 56 * 1024 * 1024


def _cparams(n_axes):
    return pltpu.CompilerParams(
        dimension_semantics=("parallel",) * n_axes, vmem_limit_bytes=VMEM_LIMIT)


def _dot(a, b):
    return jnp.dot(a, b, preferred_element_type=F32)


def _dot_nt(a, b):
    return lax.dot_general(a, b, (((1,), (1,)), ((), ())), preferred_element_type=F32)


def _norm_mod(x, g, shift, scale):
    ms = jnp.mean(x * x, axis=-1, keepdims=True)
    y = x * lax.rsqrt(ms + EPS) * g
    return y * (1.0 + scale) + shift


def _layer_norm(x, g, b):
    mu = jnp.mean(x, axis=-1, keepdims=True)
    xc = x - mu
    var = jnp.mean(xc * xc, axis=-1, keepdims=True)
    return xc * lax.rsqrt(var + EPS) * g + b


def _mod_kernel(c_ref, w_ref, b_ref, o_ref):
    c = c_ref[...]
    s = c * jax.nn.sigmoid(c)
    o_ref[0] = jnp.dot(s, w_ref[0], precision=lax.Precision.HIGHEST,
                       preferred_element_type=F32) + b_ref[0]


def _modulation(cond, ada_w, ada_b):
    depth, d, n6 = ada_w.shape
    r = cond.shape[0]
    tn = 1536
    return pl.pallas_call(
        _mod_kernel,
        grid=(depth, n6 // tn),
        in_specs=[
            pl.BlockSpec((r, d), lambda l, j: (0, 0)),
            pl.BlockSpec((1, d, tn), lambda l, j: (l, 0, j)),
            pl.BlockSpec((1, 1, tn), lambda l, j: (l, 0, j)),
        ],
        out_specs=pl.BlockSpec((1, r, tn), lambda l, j: (l, 0, j)),
        out_shape=jax.ShapeDtypeStruct((depth, r, n6), F32),
        compiler_params=_cparams(2),
        name="modulation",
    )(cond, ada_w, ada_b.reshape(depth, 1, n6))


def _in_kernel(x_ref, mod_ref, g_ref, w_ref, e_ref, gqk_ref, cos_ref, sin_ref,
               y_ref, q_ref, k_ref, v_ref, *, rope):
    x = x_ref[0]
    h = _norm_mod(x, g_ref[...], mod_ref[0, 0:1, :], mod_ref[0, 1:2, :]).astype(BF16)
    nab = 2 * CONV_DIM
    ab = _dot(h, w_ref[:, :nab])
    y_ref[0] = ab[:, :CONV_DIM] * jax.nn.sigmoid(ab[:, CONV_DIM:])

    qkv = _dot(h, w_ref[:, nab:])
    nq = N_HEADS * HEAD_DIM
    nqk = nq + N_KV_HEADS * HEAD_DIM
    e = e_ref[...]
    if rope:
        lane = lax.broadcasted_iota(jnp.int32, (1, LANES), 1)
        first_half = (lane % (HEAD_DIM // 2)) < (HEAD_DIM // 4)
        cos = cos_ref[...]
        sin = sin_ref[...]
    outs = []
    for j in range(nqk // LANES):
        t = qkv[:, j * LANES:(j + 1) * LANES]
        sq = t * t
        hi = sq.astype(BF16)
        lo = (sq - hi.astype(F32)).astype(BF16)
        ss = _dot(hi, e) + _dot(lo, e)
        tn = t * lax.rsqrt(ss * (1.0 / HEAD_DIM) + EPS) * gqk_ref[:, j * LANES:(j + 1) * LANES]
        if rope:
            quarter = HEAD_DIM // 4
            partner = jnp.where(first_half,
                                pltpu.roll(tn, LANES - quarter, 1),
                                pltpu.roll(tn, quarter, 1))
            tn = tn * cos + partner * sin
        outs.append(tn.astype(BF16))
    q_ref[0] = jnp.concatenate(outs[:nq // LANES], axis=1)
    k_ref[0] = outs[nq // LANES]
    v_ref[0] = qkv[:, nqk:].astype(BF16)


def _in_proj(x, mod, g, w, e, gqk, cos, sin, *, rope, tm):
    b, l, d = x.shape
    tm = min(tm, l)
    nq = N_HEADS * HEAD_DIM
    nkv = N_KV_HEADS * HEAD_DIM
    return pl.pallas_call(
        functools.partial(_in_kernel, rope=rope),
        grid=(b, l // tm),
        in_specs=[
            pl.BlockSpec((1, tm, d), lambda bi, i: (bi, i, 0)),
            pl.BlockSpec((1, 6, d), lambda bi, i: (bi, 0, 0)),
            pl.BlockSpec((1, d), lambda bi, i: (0, 0)),
            pl.BlockSpec(w.shape, lambda bi, i: (0, 0)),
            pl.BlockSpec(e.shape, lambda bi, i: (0, 0)),
            pl.BlockSpec(gqk.shape, lambda bi, i: (0, 0)),
            pl.BlockSpec((tm, LANES), lambda bi, i: (i, 0)),
            pl.BlockSpec((tm, LANES), lambda bi, i: (i, 0)),
        ],
        out_specs=[
            pl.BlockSpec((1, tm, CONV_DIM), lambda bi, i: (bi, i, 0)),
            pl.BlockSpec((1, tm, nq), lambda bi, i: (bi, i, 0)),
            pl.BlockSpec((1, tm, nkv), lambda bi, i: (bi, i, 0)),
            pl.BlockSpec((1, tm, nkv), lambda bi, i: (bi, i, 0)),
        ],
        out_shape=[
            jax.ShapeDtypeStruct((b, l, CONV_DIM), F32),
            jax.ShapeDtypeStruct((b, l, nq), BF16),
            jax.ShapeDtypeStruct((b, l, nkv), BF16),
            jax.ShapeDtypeStruct((b, l, nkv), BF16),
        ],
        compiler_params=_cparams(2),
        name="in_proj_rope" if rope else "in_proj_ctx",
    )(x, mod, g, w, e, gqk, cos, sin)


def _attn_kernel(sink_ref, q_ref, kc_ref, vc_ref, *rest, local, tq, seq):
    if local:
        k_ref, v_ref, o_ref = rest
        i = pl.program_id(1)
        width = tq + 2 * WINDOW
        t0 = i * tq
        start = pl.multiple_of(jnp.clip(t0 - WINDOW, 0, seq - width), WINDOW)
        kw = k_ref[0, pl.ds(start, width), :]
        vw = v_ref[0, pl.ds(start, width), :]
        qpos = t0 + lax.broadcasted_iota(jnp.int32, (tq, 1), 0)
        kpos = start + lax.broadcasted_iota(jnp.int32, (1, width), 1)
        mask = jnp.abs(qpos - kpos) <= WINDOW
    else:
        (o_ref,) = rest
    kc = kc_ref[0]
    vc = vc_ref[0]
    scale = HEAD_DIM ** -0.5
    grp = N_HEADS // N_KV_HEADS
    for h in range(N_HEADS):
        hs = slice(h * HEAD_DIM, (h + 1) * HEAD_DIM)
        ks = slice((h // grp) * HEAD_DIM, (h // grp + 1) * HEAD_DIM)
        qh = q_ref[0, :, hs]
        sink = sink_ref[h]
        s_ctx = _dot_nt(qh, kc[:, ks]) * scale
        m = jnp.maximum(jnp.max(s_ctx, axis=-1, keepdims=True), sink)
        if local:
            s_loc = jnp.where(mask, _dot_nt(qh, kw[:, ks]) * scale, NEG_INF)
            m = jnp.maximum(m, jnp.max(s_loc, axis=-1, keepdims=True))
        p_ctx = jnp.exp(s_ctx - m)
        den = jnp.sum(p_ctx, axis=-1, keepdims=True) + jnp.exp(sink - m)
        o = _dot(p_ctx.astype(BF16), vc[:, ks])
        if local:
            p_loc = jnp.exp(s_loc - m)
            den = den + jnp.sum(p_loc, axis=-1, keepdims=True)
            o = o + _dot(p_loc.astype(BF16), vw[:, ks])
        o_ref[0, :, hs] = (o / den).astype(o_ref.dtype)


def _attention(sink, q, kc, vc, k=None, v=None, *, tq):
    b, l, nq = q.shape
    lc, nkv = kc.shape[1], kc.shape[2]
    local = k is not None
    tq = min(tq, l)
    if local:
        assert l >= tq + 2 * WINDOW and tq % WINDOW == 0
    in_specs = [
        pl.BlockSpec(memory_space=pltpu.SMEM),
        pl.BlockSpec((1, tq, nq), lambda bi, i: (bi, i, 0)),
        pl.BlockSpec((1, lc, nkv), lambda bi, i: (bi, 0, 0)),
        pl.BlockSpec((1, lc, nkv), lambda bi, i: (bi, 0, 0)),
    ]
    args = [sink, q, kc, vc]
    if local:
        in_specs += [pl.BlockSpec((1, l, nkv), lambda bi, i: (bi, 0, 0))] * 2
        args += [k, v]
    return pl.pallas_call(
        functools.partial(_attn_kernel, local=local, tq=tq, seq=l),
        grid=(b, l // tq),
        in_specs=in_specs,
        out_specs=pl.BlockSpec((1, tq, nq), lambda bi, i: (bi, i, 0)),
        out_shape=jax.ShapeDtypeStruct((b, l, nq), BF16),
        compiler_params=_cparams(2),
        name="attn_window" if local else "attn_ctx",
    )(*args)


def _merge_kernel(x_ref, mod_ref, g_ref, y_ref, yp_ref, yn_ref, a_ref,
                  wuv_ref, wg_ref, gb_ref,
                  cw_ref, cb_ref, clg_ref, clb_ref, cout_ref, aout_ref,
                  slg_ref, slb_ref, sw_ref, sb_ref, sout_ref, wo_ref,
                  o_ref, ys_ref, yc_ref, *, tm, rows):
    i = pl.program_id(1)
    n = pl.num_programs(1)
    x = x_ref[0]
    h = _norm_mod(x, g_ref[...], mod_ref[0, 0:1, :], mod_ref[0, 1:2, :]).astype(BF16)
    d = x.shape[-1]

    ys_ref[0:CONV_HALO] = jnp.where(i > 0, yp_ref[0], 0.0)
    ys_ref[CONV_HALO:CONV_HALO + tm] = y_ref[0]
    ys_ref[CONV_HALO + tm:] = jnp.where(i < n - 1, yn_ref[0], 0.0)
    base = CONV_HALO - CONV_WIDTH // 2
    for r in range(0, tm, rows):
        acc = jnp.broadcast_to(cb_ref[...], (rows, CONV_DIM))
        for t in range(CONV_WIDTH):
            acc = acc + cw_ref[t:t + 1, :] * ys_ref[r + base + t:r + base + t + rows, :]
        yc_ref[r:r + rows] = acc
    yc = _layer_norm(yc_ref[...], clg_ref[...], clb_ref[...])
    yc = yc * jax.nn.sigmoid(yc)
    y_conv = _dot(yc.astype(BF16), cout_ref[...])
    m = jax.nn.sigmoid(_dot(h, wg_ref[:, 0:d]) + gb_ref[:, 0:d]) * y_conv

    y_attn = _dot(a_ref[0], aout_ref[...])
    m = m + jax.nn.sigmoid(_dot(h, wg_ref[:, d:2 * d]) + gb_ref[:, d:2 * d]) * y_attn

    uv = _dot(h, wuv_ref[...])
    u = jax.nn.gelu(uv[:, :SGU_DIM])
    v = _layer_norm(jax.nn.gelu(uv[:, SGU_DIM:]), slg_ref[...], slb_ref[...])
    gc = SGU_DIM // SGU_GROUPS
    lane = lax.broadcasted_iota(jnp.int32, (1, LANES), 1)
    low = lane < gc
    mixed = []
    for c in range(tm // SGU_CHUNK):
        cols = []
        for j in range(SGU_DIM // LANES):
            vj = v[c * SGU_CHUNK:(c + 1) * SGU_CHUNK, j * LANES:(j + 1) * LANES]
            stacked = jnp.concatenate(
                [jnp.where(low, vj, 0.0), jnp.where(low, 0.0, vj)], axis=0).astype(BF16)
            cols.append(_dot(sw_ref[j], stacked))
        mixed.append(jnp.concatenate(cols, axis=1) + sb_ref[...])
    mixed = jnp.concatenate(mixed, axis=0) if len(mixed) > 1 else mixed[0]
    y_sgu = _dot((u * mixed).astype(BF16), sout_ref[...])
    m = m + jax.nn.sigmoid(_dot(h, wg_ref[:, 2 * d:]) + gb_ref[:, 2 * d:]) * y_sgu

    o_ref[0] = x + mod_ref[0, 2:3, :] * _dot(m.astype(BF16), wo_ref[...])


def _merge(x, mod, g, y, attn, wuv, wg, gb, cw, cb, clg, clb, cout, aout,
           slg, slb, sw, sb, sout, wo, *, tm):
    b, l, d = x.shape
    tm = min(tm, l)
    hb = tm // CONV_HALO
    nhb = l // CONV_HALO

    def full(a):
        return pl.BlockSpec(a.shape, lambda bi, i: (0,) * a.ndim)

    return pl.pallas_call(
        functools.partial(_merge_kernel, tm=tm, rows=32),
        grid=(b, l // tm),
        in_specs=[
            pl.BlockSpec((1, tm, d), lambda bi, i: (bi, i, 0)),
            pl.BlockSpec((1, 6, d), lambda bi, i: (bi, 0, 0)),
            full(g),
            pl.BlockSpec((1, tm, CONV_DIM), lambda bi, i: (bi, i, 0)),
            pl.BlockSpec((1, CONV_HALO, CONV_DIM),
                         lambda bi, i: (bi, jnp.maximum(i * hb - 1, 0), 0)),
            pl.BlockSpec((1, CONV_HALO, CONV_DIM),
                         lambda bi, i: (bi, jnp.minimum((i + 1) * hb, nhb - 1), 0)),
            pl.BlockSpec((1, tm, attn.shape[-1]), lambda bi, i: (bi, i, 0)),
            full(wuv), full(wg), full(gb),
            full(cw), full(cb), full(clg), full(clb), full(cout), full(aout),
            full(slg), full(slb), full(sw), full(sb), full(sout), full(wo),
        ],
        out_specs=pl.BlockSpec((1, tm, d), lambda bi, i: (bi, i, 0)),
        out_shape=jax.ShapeDtypeStruct((b, l, d), F32),
        scratch_shapes=[
            pltpu.VMEM((tm + 2 * CONV_HALO, CONV_DIM), F32),
            pltpu.VMEM((tm, CONV_DIM), F32),
        ],
        compiler_params=_cparams(2),
        name="mixer_merge",
    )(x, mod, g, y, y, y, attn, wuv, wg, gb, cw, cb, clg, clb, cout, aout,
      slg, slb, sw, sb, sout, wo)


def _ffn_kernel(x_ref, xp_ref, xn_ref, mod_ref, g_ref, up_ref, dw_ref, db_ref, down_ref,
                o_ref, z_ref, acc_ref, *, tm, ffn):
    i = pl.program_id(1)
    n = pl.num_programs(1)
    g = g_ref[...]
    shift = mod_ref[0, 3:4, :]
    scale = mod_ref[0, 4:5, :]
    x = x_ref[0]
    h = jnp.concatenate([_norm_mod(xp_ref[0], g, shift, scale),
                         _norm_mod(x, g, shift, scale),
                         _norm_mod(xn_ref[0], g, shift, scale)], axis=0).astype(BF16)
    row = lax.broadcasted_iota(jnp.int32, (tm + 2 * FFN_HALO, 1), 0)
    valid = ((row >= FFN_HALO) | (i > 0)) & ((row < FFN_HALO + tm) | (i < n - 1))
    acc_ref[...] = jnp.zeros_like(acc_ref)
    ck = FFN_CHUNK
    for c in range(ffn // ck):
        halves = []
        for off in (c * ck, ffn + c * ck):
            z_ref[...] = jnp.where(valid, _dot(h, up_ref[:, off:off + ck]), 0.0)
            conv = db_ref[:, off:off + ck]
            for t in range(FFN_CONV_WIDTH):
                lo = FFN_HALO - FFN_CONV_WIDTH // 2 + t
                conv = conv + dw_ref[t:t + 1, off:off + ck] * z_ref[lo:lo + tm, :]
            halves.append(conv)
        a, gate = halves
        act = (a * jax.nn.sigmoid(a) * gate).astype(BF16)
        acc_ref[...] += _dot(act, down_ref[c * ck:(c + 1) * ck, :])
    o_ref[0] = x + mod_ref[0, 5:6, :] * acc_ref[...]


def _conv_ffn(x, mod, g, up, dw, db, down, *, tm):
    b, l, d = x.shape
    tm = min(tm, l)
    ffn = down.shape[0]
    hb = tm // FFN_HALO
    nhb = l // FFN_HALO

    def full(a):
        return pl.BlockSpec(a.shape, lambda bi, i: (0,) * a.ndim)

    return pl.pallas_call(
        functools.partial(_ffn_kernel, tm=tm, ffn=ffn),
        grid=(b, l // tm),
        in_specs=[
            pl.BlockSpec((1, tm, d), lambda bi, i: (bi, i, 0)),
            pl.BlockSpec((1, FFN_HALO, d), lambda bi, i: (bi, jnp.maximum(i * hb - 1, 0), 0)),
            pl.BlockSpec((1, FFN_HALO, d),
                         lambda bi, i: (bi, jnp.minimum((i + 1) * hb, nhb - 1), 0)),
            pl.BlockSpec((1, 6, d), lambda bi, i: (bi, 0, 0)),
            full(g), full(up), full(dw), full(db), full(down),
        ],
        out_specs=pl.BlockSpec((1, tm, d), lambda bi, i: (bi, i, 0)),
        out_shape=jax.ShapeDtypeStruct((b, l, d), F32),
        scratch_shapes=[
            pltpu.VMEM((tm + 2 * FFN_HALO, FFN_CHUNK), F32),
            pltpu.VMEM((tm, d), F32),
        ],
        compiler_params=_cparams(2),
        name="conv_ffn",
    )(x, x, x, mod, g, up, dw, db, down)


def _rope_tables(seq):
    quarter = HEAD_DIM // 4
    inv = jnp.power(ROPE_BASE, -jnp.arange(quarter, dtype=F32) / quarter)
    t = jnp.arange(seq)
    rows = (t // GRID_W).astype(F32)[:, None] * inv[None, :]
    cols = (t % GRID_W).astype(F32)[:, None] * inv[None, :]
    cos = jnp.concatenate([jnp.cos(rows)] * 2 + [jnp.cos(cols)] * 2, axis=1)
    sin = jnp.concatenate([-jnp.sin(rows), jnp.sin(rows), -jnp.sin(cols), jnp.sin(cols)], axis=1)
    reps = LANES // HEAD_DIM
    return jnp.tile(cos, (1, reps)), jnp.tile(sin, (1, reps))


def kernel(x, c, ctx, c_ctx, ada_w, ada_b, norm1_g, norm2_g, w_in, gate_b, conv_dw_w, conv_dw_b, conv_ln_g, conv_ln_b, conv_out, q_norm_g, k_norm_g, attn_sink, attn_out, sgu_ln_g, sgu_ln_b, sgu_w, sgu_b, sgu_out, w_o, ffn_up, ffn_dw_w, ffn_dw_b, ffn_down):
    bsz, seq, d = x.shape
    depth = ada_w.shape[0]
    n_ctx = ctx.shape[1]

    n_rows = -(-(bsz + 1) // 8) * 8
    cond = jnp.zeros((n_rows, d), F32).at[:bsz].set(c).at[bsz].set(c_ctx)
    mods = _modulation(cond, ada_w, ada_b).reshape(depth, n_rows, 6, d)

    cos, sin = _rope_tables(seq)
    ones_cs = jnp.zeros((n_ctx, LANES), F32)
    head_id = jnp.arange(LANES) // HEAD_DIM
    e = (head_id[:, None] == head_id[None, :]).astype(BF16)
    nab = 2 * CONV_DIM
    n_in1 = nab + (N_HEADS + 2 * N_KV_HEADS) * HEAD_DIM
    gc = SGU_DIM // SGU_GROUPS

    for l in range(depth):
        last = l == depth - 1
        mod_x = mods[l, :bsz]
        mod_c = jnp.broadcast_to(mods[l, bsz][None], (bsz, 6, d))
        g1 = norm1_g[l][None]
        g2 = norm2_g[l][None]
        w1 = w_in[l][:, :n_in1].astype(BF16)
        wuv = w_in[l][:, n_in1:n_in1 + 2 * SGU_DIM].astype(BF16)
        wg = w_in[l][:, n_in1 + 2 * SGU_DIM:].astype(BF16)
        gqk = jnp.concatenate([jnp.tile(q_norm_g[l], N_HEADS), jnp.tile(k_norm_g[l], N_KV_HEADS)])[None]
        sw = sgu_w[l].reshape(SGU_GROUPS // 2, 2, SGU_CHUNK, SGU_CHUNK)
        sw = jnp.concatenate([sw[:, 0], sw[:, 1]], axis=-1).astype(BF16)
        sb = jnp.repeat(sgu_b[l].T, gc, axis=1)
        merge_w = (wuv, wg, gate_b[l].reshape(1, N_BRANCH * d),
                   conv_dw_w[l], conv_dw_b[l][None], conv_ln_g[l][None], conv_ln_b[l][None],
                   conv_out[l].astype(BF16), attn_out[l].astype(BF16),
                   sgu_ln_g[l][None], sgu_ln_b[l][None], sw, sb,
                   sgu_out[l].astype(BF16), w_o[l].astype(BF16))
        ffn_w = (ffn_up[l].astype(BF16), ffn_dw_w[l], ffn_dw_b[l][None], ffn_down[l].astype(BF16))
        sink = attn_sink[l]

        yc, qc, kc, vc = _in_proj(ctx, mod_c, g1, w1, e, gqk, ones_cs, ones_cs, rope=False, tm=256)
        yx, qx, kx, vx = _in_proj(x, mod_x, g1, w1, e, gqk, cos, sin, rope=True, tm=512)
        ax = _attention(sink, qx, kc, vc, kx, vx, tq=128)
        x_mid = _merge(x, mod_x, g1, yx, ax, *merge_w, tm=256)
        x_new = _conv_ffn(x_mid, mod_x, g2, *ffn_w, tm=256)
        if not last:
            ac = _attention(sink, qc, kc, vc, tq=256)
            c_mid = _merge(ctx, mod_c, g1, yc, ac, *merge_w, tm=256)
            ctx = _conv_ffn(c_mid, mod_c, g2, *ffn_w, tm=256)
        x = x_new
    return x
```

```python
import functools

import jax
import jax.numpy as jnp
from jax import lax
from jax.experimental import pallas as pl
from jax.experimental.pallas import tpu as pltpu

F32 = jnp.float32
BF16 = jnp.bfloat16

EPS = 1e-6
NEG_INF = -1e30
ROPE_BASE = 10000.0
GRID_W = 64
HEAD_DIM = 64
N_HEADS = 8
N_KV_HEADS = 2
WINDOW = 128
CONV_DIM = 512
CONV_WIDTH = 31
SGU_DIM = 512
SGU_GROUPS = 8
SGU_CHUNK = 128
FFN_CONV_WIDTH = 3
N_BRANCH = 3

LANES = 128
SUBLANES = 8
CONV_HALO = 16
FFN_HALO = 8
FFN_CHUNK = 256
VMEM_LIMIT = 56 * 1024 * 1024

TM_IN = 512
TQ_ATTN = 256
TM_MERGE = 256
TM_FFN = 512


def _cparams(n_axes):
    return pltpu.CompilerParams(
        dimension_semantics=("parallel",) * n_axes, vmem_limit_bytes=VMEM_LIMIT)


def _dot(a, b):
    return jnp.dot(a, b, preferred_element_type=F32)


def _dot_nt(a, b):
    return lax.dot_general(a, b, (((1,), (1,)), ((), ())), preferred_element_type=F32)


def _norm_mod(x, g, shift, scale):
    ms = jnp.mean(x * x, axis=-1, keepdims=True)
    y = x * lax.rsqrt(ms + EPS) * g
    return y * (1.0 + scale) + shift


def _layer_norm(x, g, b):
    mu = jnp.mean(x, axis=-1, keepdims=True)
    xc = x - mu
    var = jnp.mean(xc * xc, axis=-1, keepdims=True)
    return xc * lax.rsqrt(var + EPS) * g + b


def _low_half_lanes():
    return lax.broadcasted_iota(jnp.int32, (1, LANES), 1) < (LANES // 2)


def _full_spec(a):
    return pl.BlockSpec(a.shape, lambda bi, i: (0,) * a.ndim)


def _mod_kernel(c_ref, w_ref, b_ref, o_ref):
    c = c_ref[...]
    s = c * jax.nn.sigmoid(c)
    o_ref[0] = jnp.dot(s, w_ref[0], precision=lax.Precision.HIGHEST,
                       preferred_element_type=F32) + b_ref[0]


def _modulation(cond, ada_w, ada_b):
    depth, d, n6 = ada_w.shape
    r = cond.shape[0]
    tn = 1536
    return pl.pallas_call(
        _mod_kernel,
        grid=(depth, n6 // tn),
        in_specs=[
            pl.BlockSpec((r, d), lambda l, j: (0, 0)),
            pl.BlockSpec((1, d, tn), lambda l, j: (l, 0, j)),
            pl.BlockSpec((1, 1, tn), lambda l, j: (l, 0, j)),
        ],
        out_specs=pl.BlockSpec((1, r, tn), lambda l, j: (l, 0, j)),
        out_shape=jax.ShapeDtypeStruct((depth, r, n6), F32),
        compiler_params=_cparams(2),
        name="modulation",
    )(cond, ada_w, ada_b.reshape(depth, 1, n6))


def _in_kernel(x_ref, mod_ref, g_ref, w_ref, e_ref, gqk_ref, cos_ref, sin_ref,
               y_ref, q_ref, k_ref, v_ref, *, rope):
    x = x_ref[0]
    h = _norm_mod(x, g_ref[...], mod_ref[0, 0:1, :], mod_ref[0, 1:2, :]).astype(BF16)
    nab = 2 * CONV_DIM
    ab = _dot(h, w_ref[:, :nab])
    y_ref[0] = ab[:, :CONV_DIM] * jax.nn.sigmoid(ab[:, CONV_DIM:])

    qkv = _dot(h, w_ref[:, nab:])
    nq = N_HEADS * HEAD_DIM
    nqk = nq + N_KV_HEADS * HEAD_DIM
    e = e_ref[...]
    if rope:
        lane = lax.broadcasted_iota(jnp.int32, (1, LANES), 1)
        first_half = (lane % (HEAD_DIM // 2)) < (HEAD_DIM // 4)
        cos = cos_ref[...]
        sin = sin_ref[...]
    outs = []
    for j in range(nqk // LANES):
        t = qkv[:, j * LANES:(j + 1) * LANES]
        sq = t * t
        hi = sq.astype(BF16)
        lo = (sq - hi.astype(F32)).astype(BF16)
        ss = _dot(hi, e) + _dot(lo, e)
        tn = t * lax.rsqrt(ss * (1.0 / HEAD_DIM) + EPS) * gqk_ref[:, j * LANES:(j + 1) * LANES]
        if rope:
            quarter = HEAD_DIM // 4
            partner = jnp.where(first_half,
                                pltpu.roll(tn, LANES - quarter, 1),
                                pltpu.roll(tn, quarter, 1))
            tn = tn * cos + partner * sin
        outs.append(tn)
    q_ref[0] = jnp.concatenate(outs[:nq // LANES], axis=1).astype(BF16)
    low = _low_half_lanes()
    kk = outs[nq // LANES]
    kr = pltpu.roll(kk, LANES // 2, 1)
    k_ref[0] = jnp.concatenate([jnp.where(low, kk, kr), jnp.where(low, kr, kk)], axis=1).astype(BF16)
    vv = qkv[:, nqk:]
    vr = pltpu.roll(vv, LANES // 2, 1)
    v_ref[0] = jnp.concatenate([jnp.where(low, vv, 0.0), jnp.where(low, 0.0, vr),
                                jnp.where(low, vr, 0.0), jnp.where(low, 0.0, vv)], axis=1).astype(BF16)


def _in_proj(x, mod, g, w, e, gqk, cos, sin, *, rope):
    b, l, d = x.shape
    tm = min(TM_IN, l)
    nq = N_HEADS * HEAD_DIM
    nk = 2 * N_KV_HEADS * HEAD_DIM
    nv = 4 * N_KV_HEADS * HEAD_DIM
    return pl.pallas_call(
        functools.partial(_in_kernel, rope=rope),
        grid=(b, l // tm),
        in_specs=[
            pl.BlockSpec((1, tm, d), lambda bi, i: (bi, i, 0)),
            pl.BlockSpec((1, 6, d), lambda bi, i: (bi, 0, 0)),
            _full_spec(g), _full_spec(w), _full_spec(e), _full_spec(gqk),
            pl.BlockSpec((tm, LANES), lambda bi, i: (i, 0)),
            pl.BlockSpec((tm, LANES), lambda bi, i: (i, 0)),
        ],
        out_specs=[
            pl.BlockSpec((1, tm, CONV_DIM), lambda bi, i: (bi, i, 0)),
            pl.BlockSpec((1, tm, nq), lambda bi, i: (bi, i, 0)),
            pl.BlockSpec((1, tm, nk), lambda bi, i: (bi, i, 0)),
            pl.BlockSpec((1, tm, nv), lambda bi, i: (bi, i, 0)),
        ],
        out_shape=[
            jax.ShapeDtypeStruct((b, l, CONV_DIM), F32),
            jax.ShapeDtypeStruct((b, l, nq), BF16),
            jax.ShapeDtypeStruct((b, l, nk), BF16),
            jax.ShapeDtypeStruct((b, l, nv), BF16),
        ],
        compiler_params=_cparams(2),
        name="in_proj_rope" if rope else "in_proj_ctx",
    )(x, mod, g, w, e, gqk, cos, sin)


def _attn_kernel(sink_ref, q_ref, kc_ref, vc_ref, *rest, local, tq, seq):
    if local:
        k_ref, v_ref, o_ref = rest
        i = pl.program_id(1)
        width = tq + 2 * WINDOW
        t0 = i * tq
        start = pl.multiple_of(jnp.clip(t0 - WINDOW, 0, seq - width), WINDOW)
        qpos = t0 + lax.broadcasted_iota(jnp.int32, (tq, 1), 0)
        kpos = start + lax.broadcasted_iota(jnp.int32, (1, width), 1)
        mask = jnp.abs(qpos - kpos) <= WINDOW
    else:
        (o_ref,) = rest
    low = _low_half_lanes()
    grp = N_HEADS // N_KV_HEADS
    for j in range(N_HEADS // 2):
        hk = (2 * j) // grp
        qp = q_ref[0, :, j * LANES:(j + 1) * LANES]
        kc = kc_ref[0, :, hk * LANES:(hk + 1) * LANES]
        if local:
            kw = k_ref[0, pl.ds(start, width), hk * LANES:(hk + 1) * LANES]
        o_pair = None
        for s in range(2):
            qh = jnp.where(low, qp, 0.0) if s == 0 else jnp.where(low, 0.0, qp)
            qh = qh.astype(BF16)
            sink = sink_ref[2 * j + s]
            vs = slice((2 * hk + s) * LANES, (2 * hk + s + 1) * LANES)
            s_ctx = _dot_nt(qh, kc)
            m = jnp.maximum(jnp.max(s_ctx, axis=-1, keepdims=True), sink)
            if local:
                s_loc = jnp.where(mask, _dot_nt(qh, kw), NEG_INF)
                m = jnp.maximum(m, jnp.max(s_loc, axis=-1, keepdims=True))
            p_ctx = jnp.exp(s_ctx - m)
            den = jnp.sum(p_ctx, axis=-1, keepdims=True) + jnp.exp(sink - m)
            o = _dot(p_ctx.astype(BF16), vc_ref[0, :, vs])
            if local:
                p_loc = jnp.exp(s_loc - m)
                den = den + jnp.sum(p_loc, axis=-1, keepdims=True)
                o = o + _dot(p_loc.astype(BF16), v_ref[0, pl.ds(start, width), vs])
            o = o / den
            o_pair = o if o_pair is None else o_pair + o
        o_ref[0, :, j * LANES:(j + 1) * LANES] = o_pair.astype(o_ref.dtype)


def _attention(sink, q, kc, vc, k=None, v=None):
    b, l, nq = q.shape
    lc = kc.shape[1]
    local = k is not None
    tq = min(TQ_ATTN, l)
    if local:
        assert l >= tq + 2 * WINDOW and tq % WINDOW == 0
    in_specs = [
        pl.BlockSpec(memory_space=pltpu.SMEM),
        pl.BlockSpec((1, tq, nq), lambda bi, i: (bi, i, 0)),
        pl.BlockSpec((1, lc, kc.shape[2]), lambda bi, i: (bi, 0, 0)),
        pl.BlockSpec((1, lc, vc.shape[2]), lambda bi, i: (bi, 0, 0)),
    ]
    args = [sink, q, kc, vc]
    if local:
        in_specs += [pl.BlockSpec((1, l, k.shape[2]), lambda bi, i: (bi, 0, 0)),
                     pl.BlockSpec((1, l, v.shape[2]), lambda bi, i: (bi, 0, 0))]
        args += [k, v]
    return pl.pallas_call(
        functools.partial(_attn_kernel, local=local, tq=tq, seq=l),
        grid=(b, l // tq),
        in_specs=in_specs,
        out_specs=pl.BlockSpec((1, tq, nq), lambda bi, i: (bi, i, 0)),
        out_shape=jax.ShapeDtypeStruct((b, l, nq), BF16),
        compiler_params=_cparams(2),
        name="attn_window" if local else "attn_ctx",
    )(*args)


def _merge_kernel(x_ref, mod_ref, g_ref, y_ref, yp_ref, yn_ref, a_ref,
                  wuv_ref, wg_ref, gb_ref,
                  cw_ref, cb_ref, clg_ref, clb_ref, cout_ref, aout_ref,
                  slg_ref, slb_ref, sw_ref, sb_ref, sout_ref, wo_ref,
                  o_ref, ys_ref, yc_ref, *, tm, rows):
    i = pl.program_id(1)
    n = pl.num_programs(1)
    x = x_ref[0]
    h = _norm_mod(x, g_ref[...], mod_ref[0, 0:1, :], mod_ref[0, 1:2, :]).astype(BF16)
    d = x.shape[-1]

    ext = tm + 2 * CONV_HALO
    ys = jnp.concatenate([jnp.where(i > 0, yp_ref[0], 0.0), y_ref[0],
                          jnp.where(i < n - 1, yn_ref[0], 0.0)], axis=0)
    ys_ref[0] = ys
    for r in range(1, SUBLANES):
        ys_ref[r] = pltpu.roll(ys, ext - r, 0)
    base = CONV_HALO - CONV_WIDTH // 2
    for r0 in range(0, tm, rows):
        acc = jnp.broadcast_to(cb_ref[...], (rows, CONV_DIM))
        for t in range(CONV_WIDTH):
            off = r0 + base + t
            lo = off - off % SUBLANES
            acc = acc + cw_ref[t:t + 1, :] * ys_ref[off % SUBLANES, lo:lo + rows, :]
        yc_ref[r0:r0 + rows] = acc
    yc = _layer_norm(yc_ref[...], clg_ref[...], clb_ref[...])
    yc = yc * jax.nn.sigmoid(yc)
    y_conv = _dot(yc.astype(BF16), cout_ref[...])
    m = jax.nn.sigmoid(_dot(h, wg_ref[:, 0:d]) + gb_ref[:, 0:d]) * y_conv

    y_attn = _dot(a_ref[0], aout_ref[...])
    m = m + jax.nn.sigmoid(_dot(h, wg_ref[:, d:2 * d]) + gb_ref[:, d:2 * d]) * y_attn

    uv = _dot(h, wuv_ref[...])
    u = jax.nn.gelu(uv[:, :SGU_DIM])
    v = _layer_norm(jax.nn.gelu(uv[:, SGU_DIM:]), slg_ref[...], slb_ref[...])
    low = _low_half_lanes()
    mixed = []
    for c in range(tm // SGU_CHUNK):
        cols = []
        for j in range(SGU_DIM // LANES):
            vj = v[c * SGU_CHUNK:(c + 1) * SGU_CHUNK, j * LANES:(j + 1) * LANES]
            stacked = jnp.concatenate(
                [jnp.where(low, vj, 0.0), jnp.where(low, 0.0, vj)], axis=0).astype(BF16)
            cols.append(_dot(sw_ref[j], stacked))
        mixed.append(jnp.concatenate(cols, axis=1) + sb_ref[...])
    mixed = jnp.concatenate(mixed, axis=0) if len(mixed) > 1 else mixed[0]
    y_sgu = _dot((u * mixed).astype(BF16), sout_ref[...])
    m = m + jax.nn.sigmoid(_dot(h, wg_ref[:, 2 * d:]) + gb_ref[:, 2 * d:]) * y_sgu

    o_ref[0] = x + mod_ref[0, 2:3, :] * _dot(m.astype(BF16), wo_ref[...])


def _merge(x, mod, g, y, attn, wuv, wg, gb, cw, cb, clg, clb, cout, aout,
           slg, slb, sw, sb, sout, wo):
    b, l, d = x.shape
    tm = min(TM_MERGE, l)
    hb = tm // CONV_HALO
    nhb = l // CONV_HALO
    weights = (wuv, wg, gb, cw, cb, clg, clb, cout, aout, slg, slb, sw, sb, sout, wo)
    return pl.pallas_call(
        functools.partial(_merge_kernel, tm=tm, rows=64),
        grid=(b, l // tm),
        in_specs=[
            pl.BlockSpec((1, tm, d), lambda bi, i: (bi, i, 0)),
            pl.BlockSpec((1, 6, d), lambda bi, i: (bi, 0, 0)),
            _full_spec(g),
            pl.BlockSpec((1, tm, CONV_DIM), lambda bi, i: (bi, i, 0)),
            pl.BlockSpec((1, CONV_HALO, CONV_DIM),
                         lambda bi, i: (bi, jnp.maximum(i * hb - 1, 0), 0)),
            pl.BlockSpec((1, CONV_HALO, CONV_DIM),
                         lambda bi, i: (bi, jnp.minimum((i + 1) * hb, nhb - 1), 0)),
            pl.BlockSpec((1, tm, attn.shape[-1]), lambda bi, i: (bi, i, 0)),
        ] + [_full_spec(a) for a in weights],
        out_specs=pl.BlockSpec((1, tm, d), lambda bi, i: (bi, i, 0)),
        out_shape=jax.ShapeDtypeStruct((b, l, d), F32),
        scratch_shapes=[
            pltpu.VMEM((SUBLANES, tm + 2 * CONV_HALO, CONV_DIM), F32),
            pltpu.VMEM((tm, CONV_DIM), F32),
        ],
        compiler_params=_cparams(2),
        name="mixer_merge",
    )(x, mod, g, y, y, y, attn, *weights)


def _ffn_kernel(x_ref, xp_ref, xn_ref, mod_ref, g_ref, up_ref, dw_ref, db_ref, down_ref,
                o_ref, acc_ref, *, tm, ffn):
    i = pl.program_id(1)
    n = pl.num_programs(1)
    g = g_ref[...]
    shift = mod_ref[0, 3:4, :]
    scale = mod_ref[0, 4:5, :]
    x = x_ref[0]
    h = jnp.concatenate([jnp.where(i > 0, _norm_mod(xp_ref[0], g, shift, scale), 0.0),
                         _norm_mod(x, g, shift, scale),
                         jnp.where(i < n - 1, _norm_mod(xn_ref[0], g, shift, scale), 0.0)],
                        axis=0).astype(BF16)
    ext = tm + 2 * FFN_HALO
    ck = FFN_CHUNK
    for c in range(ffn // ck):
        halves = []
        for off in (c * ck, ffn + c * ck):
            z = _dot(h, up_ref[:, off:off + ck])
            conv = (db_ref[:, off:off + ck]
                    + dw_ref[0:1, off:off + ck] * pltpu.roll(z, 1, 0)[FFN_HALO:FFN_HALO + tm]
                    + dw_ref[1:2, off:off + ck] * z[FFN_HALO:FFN_HALO + tm]
                    + dw_ref[2:3, off:off + ck] * pltpu.roll(z, ext - 1, 0)[FFN_HALO:FFN_HALO + tm])
            halves.append(conv)
        a, gate = halves
        act = (a * jax.nn.sigmoid(a) * gate).astype(BF16)
        part = _dot(act, down_ref[c * ck:(c + 1) * ck, :])
        if c == 0:
            acc_ref[...] = part
        else:
            acc_ref[...] += part
    o_ref[0] = x + mod_ref[0, 5:6, :] * acc_ref[...]


def _conv_ffn(x, mod, g, up, dw, db, down):
    b, l, d = x.shape
    tm = min(TM_FFN, l)
    ffn = down.shape[0]
    hb = tm // FFN_HALO
    nhb = l // FFN_HALO
    return pl.pallas_call(
        functools.partial(_ffn_kernel, tm=tm, ffn=ffn),
        grid=(b, l // tm),
        in_specs=[
            pl.BlockSpec((1, tm, d), lambda bi, i: (bi, i, 0)),
            pl.BlockSpec((1, FFN_HALO, d), lambda bi, i: (bi, jnp.maximum(i * hb - 1, 0), 0)),
            pl.BlockSpec((1, FFN_HALO, d),
                         lambda bi, i: (bi, jnp.minimum((i + 1) * hb, nhb - 1), 0)),
            pl.BlockSpec((1, 6, d), lambda bi, i: (bi, 0, 0)),
            _full_spec(g), _full_spec(up), _full_spec(dw), _full_spec(db), _full_spec(down),
        ],
        out_specs=pl.BlockSpec((1, tm, d), lambda bi, i: (bi, i, 0)),
        out_shape=jax.ShapeDtypeStruct((b, l, d), F32),
        scratch_shapes=[pltpu.VMEM((tm, d), F32)],
        compiler_params=_cparams(2),
        name="conv_ffn",
    )(x, x, x, mod, g, up, dw, db, down)


def _rope_tables(seq):
    quarter = HEAD_DIM // 4
    inv = jnp.power(ROPE_BASE, -jnp.arange(quarter, dtype=F32) / quarter)
    t = jnp.arange(seq)
    rows = (t // GRID_W).astype(F32)[:, None] * inv[None, :]
    cols = (t % GRID_W).astype(F32)[:, None] * inv[None, :]
    cos = jnp.concatenate([jnp.cos(rows)] * 2 + [jnp.cos(cols)] * 2, axis=1)
    sin = jnp.concatenate([-jnp.sin(rows), jnp.sin(rows), -jnp.sin(cols), jnp.sin(cols)], axis=1)
    reps = LANES // HEAD_DIM
    return jnp.tile(cos, (1, reps)), jnp.tile(sin, (1, reps))


def kernel(x, c, ctx, c_ctx, ada_w, ada_b, norm1_g, norm2_g, w_in, gate_b, conv_dw_w, conv_dw_b, conv_ln_g, conv_ln_b, conv_out, q_norm_g, k_norm_g, attn_sink, attn_out, sgu_ln_g, sgu_ln_b, sgu_w, sgu_b, sgu_out, w_o, ffn_up, ffn_dw_w, ffn_dw_b, ffn_down):
    bsz, seq, d = x.shape
    depth = ada_w.shape[0]
    n_ctx = ctx.shape[1]

    n_rows = -(-(bsz + 1) // SUBLANES) * SUBLANES
    cond = jnp.zeros((n_rows, d), F32).at[:bsz].set(c).at[bsz].set(c_ctx)
    mods = _modulation(cond, ada_w, ada_b).reshape(depth, n_rows, 6, d)

    cos, sin = _rope_tables(seq)
    no_rope = jnp.zeros((n_ctx, LANES), F32)
    head_id = jnp.arange(LANES) // HEAD_DIM
    e = (head_id[:, None] == head_id[None, :]).astype(BF16)
    nab = 2 * CONV_DIM
    n_in1 = nab + (N_HEADS + 2 * N_KV_HEADS) * HEAD_DIM
    gc = SGU_DIM // SGU_GROUPS

    for l in range(depth):
        last = l == depth - 1
        mod_x = mods[l, :bsz]
        mod_c = jnp.broadcast_to(mods[l, bsz][None], (bsz, 6, d))
        g1 = norm1_g[l][None]
        g2 = norm2_g[l][None]
        w1 = w_in[l][:, :n_in1].astype(BF16)
        wuv = w_in[l][:, n_in1:n_in1 + 2 * SGU_DIM].astype(BF16)
        wg = w_in[l][:, n_in1 + 2 * SGU_DIM:].astype(BF16)
        gqk = jnp.concatenate([jnp.tile(q_norm_g[l] * HEAD_DIM ** -0.5, N_HEADS),
                               jnp.tile(k_norm_g[l], N_KV_HEADS)])[None]
        sw = sgu_w[l].reshape(SGU_GROUPS // 2, 2, SGU_CHUNK, SGU_CHUNK)
        sw = jnp.concatenate([sw[:, 0], sw[:, 1]], axis=-1).astype(BF16)
        sb = jnp.repeat(sgu_b[l].T, gc, axis=1)
        merge_w = (wuv, wg, gate_b[l].reshape(1, N_BRANCH * d),
                   conv_dw_w[l], conv_dw_b[l][None], conv_ln_g[l][None], conv_ln_b[l][None],
                   conv_out[l].astype(BF16), attn_out[l].astype(BF16),
                   sgu_ln_g[l][None], sgu_ln_b[l][None], sw, sb,
                   sgu_out[l].astype(BF16), w_o[l].astype(BF16))
        ffn_w = (ffn_up[l].astype(BF16), ffn_dw_w[l], ffn_dw_b[l][None], ffn_down[l].astype(BF16))
        sink = attn_sink[l]

        yc, qc, kc, vc = _in_proj(ctx, mod_c, g1, w1, e, gqk, no_rope, no_rope, rope=False)
        yx, qx, kx, vx = _in_proj(x, mod_x, g1, w1, e, gqk, cos, sin, rope=True)
        ax = _attention(sink, qx, kc, vc, kx, vx)
        x_mid = _merge(x, mod_x, g1, yx, ax, *merge_w)
        x_new = _conv_ffn(x_mid, mod_x, g2, *ffn_w)
        if not last:
            ac = _attention(sink, qc, kc, vc)
            c_mid = _merge(ctx, mod_c, g1, yc, ac, *merge_w)
            ctx = _conv_ffn(c_mid, mod_c, g2, *ffn_w)
        x = x_new
    return x
```

```python
import functools

import jax
import jax.numpy as jnp
from jax import lax
from jax.experimental import pallas as pl
from jax.experimental.pallas import tpu as pltpu

F32 = jnp.float32
BF16 = jnp.bfloat16

EPS = 1e-6
NEG_INF = -1e30
ROPE_BASE = 10000.0
LOG2E = 1.4426950408889634
GRID_W = 64
HEAD_DIM = 64
N_HEADS = 8
N_KV_HEADS = 2
WINDOW = 128
CONV_DIM = 512
CONV_WIDTH = 31
SGU_DIM = 512
SGU_GROUPS = 8
SGU_CHUNK = 128
FFN_CONV_WIDTH = 3
N_BRANCH = 3

LANES = 128
SUBLANES = 8
CONV_HALO = 16
FFN_HALO = 8
FFN_CHUNK = 256
FFN_ROWS = 256
CONV_ROWS = 64
VMEM_LIMIT = 56 * 1024 * 1024

TM_IN = 512
TQ_ATTN = 256
TM_MERGE = 256
TM_FFN = 512


def _cparams(n_axes, flags=None):
    return pltpu.CompilerParams(
        dimension_semantics=("parallel",) * n_axes, vmem_limit_bytes=VMEM_LIMIT, flags=flags)


def _dot(a, b):
    return jnp.dot(a, b, preferred_element_type=F32)


def _dot_nt(a, b):
    return lax.dot_general(a, b, (((1,), (1,)), ((), ())), preferred_element_type=F32)


def _norm_mod(x, g, shift, scale):
    ms = jnp.mean(x * x, axis=-1, keepdims=True)
    y = x * lax.rsqrt(ms + EPS) * g
    return y * (1.0 + scale) + shift


def _layer_norm(x, g, b):
    mu = jnp.mean(x, axis=-1, keepdims=True)
    xc = x - mu
    var = jnp.mean(xc * xc, axis=-1, keepdims=True)
    return xc * lax.rsqrt(var + EPS) * g + b


def _low_half_lanes():
    return lax.broadcasted_iota(jnp.int32, (1, LANES), 1) < (LANES // 2)


def _full_spec(a):
    return pl.BlockSpec(a.shape, lambda bi, i: (0,) * a.ndim)


def _mod_kernel(c_ref, w_ref, b_ref, o_ref):
    c = c_ref[...]
    s = c * jax.nn.sigmoid(c)
    o_ref[0] = jnp.dot(s, w_ref[0], precision=lax.Precision.HIGHEST,
                       preferred_element_type=F32) + b_ref[0]


def _modulation(cond, ada_w, ada_b):
    depth, d, n6 = ada_w.shape
    r = cond.shape[0]
    tn = 1536
    return pl.pallas_call(
        _mod_kernel,
        grid=(depth, n6 // tn),
        in_specs=[
            pl.BlockSpec((r, d), lambda l, j: (0, 0)),
            pl.BlockSpec((1, d, tn), lambda l, j: (l, 0, j)),
            pl.BlockSpec((1, 1, tn), lambda l, j: (l, 0, j)),
        ],
        out_specs=pl.BlockSpec((1, r, tn), lambda l, j: (l, 0, j)),
        out_shape=jax.ShapeDtypeStruct((depth, r, n6), F32),
        compiler_params=_cparams(2),
        name="modulation",
    )(cond, ada_w, ada_b.reshape(depth, 1, n6))


def _in_kernel(x_ref, mod_ref, g_ref, w_ref, e_ref, gqk_ref, cos_ref, sin_ref,
               y_ref, q_ref, k_ref, v_ref, *, rope):
    x = x_ref[0]
    h = _norm_mod(x, g_ref[...], mod_ref[0, 0:1, :], mod_ref[0, 1:2, :]).astype(BF16)
    nab = 2 * CONV_DIM
    ab = _dot(h, w_ref[:, :nab])
    y_ref[0] = ab[:, :CONV_DIM] * jax.nn.sigmoid(ab[:, CONV_DIM:])

    qkv = _dot(h, w_ref[:, nab:])
    nq = N_HEADS * HEAD_DIM
    nqk = nq + N_KV_HEADS * HEAD_DIM
    e = e_ref[...]
    if rope:
        lane = lax.broadcasted_iota(jnp.int32, (1, LANES), 1)
        first_half = (lane % (HEAD_DIM // 2)) < (HEAD_DIM // 4)
        cos = cos_ref[...]
        sin = sin_ref[...]
    outs = []
    for j in range(nqk // LANES):
        t = qkv[:, j * LANES:(j + 1) * LANES]
        sq = t * t
        hi = sq.astype(BF16)
        lo = (sq - hi.astype(F32)).astype(BF16)
        ss = _dot(jnp.concatenate([hi, lo], axis=1), e)
        tn = t * lax.rsqrt(ss * (1.0 / HEAD_DIM) + EPS) * gqk_ref[:, j * LANES:(j + 1) * LANES]
        if rope:
            quarter = HEAD_DIM // 4
            partner = jnp.where(first_half,
                                pltpu.roll(tn, LANES - quarter, 1),
                                pltpu.roll(tn, quarter, 1))
            tn = tn * cos + partner * sin
        outs.append(tn)
    q_ref[0] = jnp.concatenate(outs[:nq // LANES], axis=1).astype(BF16)
    low = _low_half_lanes()
    kk = outs[nq // LANES]
    kr = pltpu.roll(kk, LANES // 2, 1)
    k_ref[0] = jnp.concatenate([jnp.where(low, kk, kr), jnp.where(low, kr, kk)], axis=1).astype(BF16)
    vv = qkv[:, nqk:]
    vr = pltpu.roll(vv, LANES // 2, 1)
    v_ref[0] = jnp.concatenate([jnp.where(low, vv, 0.0), jnp.where(low, 0.0, vr),
                                jnp.where(low, vr, 0.0), jnp.where(low, 0.0, vv)], axis=1).astype(BF16)


def _in_proj(x, mod, g, w, e, gqk, cos, sin, *, rope):
    b, l, d = x.shape
    tm = min(TM_IN, l)
    nq = N_HEADS * HEAD_DIM
    nk = 2 * N_KV_HEADS * HEAD_DIM
    nv = 4 * N_KV_HEADS * HEAD_DIM
    return pl.pallas_call(
        functools.partial(_in_kernel, rope=rope),
        grid=(b, l // tm),
        in_specs=[
            pl.BlockSpec((1, tm, d), lambda bi, i: (bi, i, 0)),
            pl.BlockSpec((1, 6, d), lambda bi, i: (bi, 0, 0)),
            _full_spec(g), _full_spec(w), _full_spec(e), _full_spec(gqk),
            pl.BlockSpec((tm, LANES), lambda bi, i: (i, 0)),
            pl.BlockSpec((tm, LANES), lambda bi, i: (i, 0)),
        ],
        out_specs=[
            pl.BlockSpec((1, tm, CONV_DIM), lambda bi, i: (bi, i, 0)),
            pl.BlockSpec((1, tm, nq), lambda bi, i: (bi, i, 0)),
            pl.BlockSpec((1, tm, nk), lambda bi, i: (bi, i, 0)),
            pl.BlockSpec((1, tm, nv), lambda bi, i: (bi, i, 0)),
        ],
        out_shape=[
            jax.ShapeDtypeStruct((b, l, CONV_DIM), F32),
            jax.ShapeDtypeStruct((b, l, nq), BF16),
            jax.ShapeDtypeStruct((b, l, nk), BF16),
            jax.ShapeDtypeStruct((b, l, nv), BF16),
        ],
        compiler_params=_cparams(2),
        name="in_proj_rope" if rope else "in_proj_ctx",
    )(x, mod, g, w, e, gqk, cos, sin)


def _attn_kernel(sink_ref, q_ref, kc_ref, vc_ref, *rest, local, tq, seq):
    if local:
        k_ref, v_ref, o_ref, bias_ref = rest
        i = pl.program_id(1)
        width = tq + 2 * WINDOW
        t0 = i * tq
        start = pl.multiple_of(jnp.clip(t0 - WINDOW, 0, seq - width), WINDOW)
        qpos = t0 + lax.broadcasted_iota(jnp.int32, (tq, 1), 0)
        kpos = start + lax.broadcasted_iota(jnp.int32, (1, width), 1)
        bias_ref[...] = jnp.where(jnp.abs(qpos - kpos) <= WINDOW, 0.0, NEG_INF)
    else:
        (o_ref,) = rest
    low = _low_half_lanes()
    grp = N_HEADS // N_KV_HEADS
    n_ctx = kc_ref.shape[1]

    def scores(h):
        hk = h // grp
        qp = q_ref[0, :, (h // 2) * LANES:(h // 2 + 1) * LANES]
        qh = jnp.where(low, qp, 0.0) if h % 2 == 0 else jnp.where(low, 0.0, qp)
        s = _dot_nt(qh, kc_ref[0, :, hk * LANES:(hk + 1) * LANES])
        if local:
            s_loc = _dot_nt(qh, k_ref[0, pl.ds(start, width), hk * LANES:(hk + 1) * LANES])
            s = jnp.concatenate([s, s_loc + bias_ref[...]], axis=1)
        return s

    def softmax(h, s):
        sink = sink_ref[h]
        m = jnp.maximum(jnp.max(s, axis=-1, keepdims=True), sink)
        p = jnp.exp2(s - m)
        den = jnp.sum(p, axis=-1, keepdims=True) + jnp.exp2(sink - m)
        return p.astype(BF16), 1.0 / den

    def values(h, p, inv):
        vs = slice((2 * (h // grp) + h % 2) * LANES, (2 * (h // grp) + h % 2 + 1) * LANES)
        o = _dot(p[:, :n_ctx], vc_ref[0, :, vs])
        if local:
            o = o + _dot(p[:, n_ctx:], v_ref[0, pl.ds(start, width), vs])
        return o * inv

    s = scores(0)
    o_pair = None
    for h in range(N_HEADS):
        s_next = scores(h + 1) if h + 1 < N_HEADS else None
        p, inv = softmax(h, s)
        o = values(h, p, inv)
        if h % 2 == 0:
            o_pair = o
        else:
            o_ref[0, :, (h // 2) * LANES:(h // 2 + 1) * LANES] = (o_pair + o).astype(o_ref.dtype)
        s = s_next


def _attention(sink, q, kc, vc, k=None, v=None):
    b, l, nq = q.shape
    lc = kc.shape[1]
    local = k is not None
    tq = min(TQ_ATTN, l)
    if local:
        assert l >= tq + 2 * WINDOW and tq % WINDOW == 0
    in_specs = [
        pl.BlockSpec(memory_space=pltpu.SMEM),
        pl.BlockSpec((1, tq, nq), lambda bi, i: (bi, i, 0)),
        pl.BlockSpec((1, lc, kc.shape[2]), lambda bi, i: (bi, 0, 0)),
        pl.BlockSpec((1, lc, vc.shape[2]), lambda bi, i: (bi, 0, 0)),
    ]
    args = [sink, q, kc, vc]
    if local:
        in_specs += [pl.BlockSpec((1, l, k.shape[2]), lambda bi, i: (bi, 0, 0)),
                     pl.BlockSpec((1, l, v.shape[2]), lambda bi, i: (bi, 0, 0))]
        args += [k, v]
    return pl.pallas_call(
        functools.partial(_attn_kernel, local=local, tq=tq, seq=l),
        grid=(b, l // tq),
        in_specs=in_specs,
        out_specs=pl.BlockSpec((1, tq, nq), lambda bi, i: (bi, i, 0)),
        out_shape=jax.ShapeDtypeStruct((b, l, nq), BF16),
        scratch_shapes=[pltpu.VMEM((tq, tq + 2 * WINDOW), F32)] if local else [],
        compiler_params=_cparams(2),
        name="attn_window" if local else "attn_ctx",
    )(*args)


def _merge_kernel(x_ref, mod_ref, g_ref, y_ref, yp_ref, yn_ref, a_ref,
                  w4_ref, b4_ref,
                  cw_ref, cb_ref, clg_ref, clb_ref, cout_ref, aout_ref,
                  slg_ref, slb_ref, sw_ref, sb_ref, sout_ref, wo_ref,
                  o_ref, h_ref, ys_ref, yc_ref, *, tm, rows):
    i = pl.program_id(1)
    n = pl.num_programs(1)
    x = x_ref[0]
    h_ref[...] = _norm_mod(x, g_ref[...], mod_ref[0, 0:1, :], mod_ref[0, 1:2, :]).astype(BF16)

    ext = tm + 2 * CONV_HALO
    ys = jnp.concatenate([jnp.where(i > 0, yp_ref[0], 0.0), y_ref[0],
                          jnp.where(i < n - 1, yn_ref[0], 0.0)], axis=0)
    ys_ref[0] = ys
    for r in range(1, SUBLANES):
        ys_ref[r] = pltpu.roll(ys, ext - r, 0)
    base = CONV_HALO - CONV_WIDTH // 2
    n_pre = w4_ref.shape[0]
    pre = [_dot(h_ref[...], w4_ref[k]) + b4_ref[k] for k in range(n_pre)]
    uv, gates = pre[0], pre[1:]
    y_attn = _dot(a_ref[0], aout_ref[...])

    for r0 in range(0, tm, rows):
        acc = jnp.broadcast_to(cb_ref[...], (rows, CONV_DIM))
        for t in range(CONV_WIDTH):
            off = r0 + base + t
            lo = off - off % SUBLANES
            acc = acc + cw_ref[t:t + 1, :] * ys_ref[off % SUBLANES, lo:lo + rows, :]
        yc_ref[r0:r0 + rows, :] = acc
    yc = _layer_norm(yc_ref[...], clg_ref[...], clb_ref[...])
    yc = yc * jax.nn.sigmoid(yc)
    y_conv = _dot(yc.astype(BF16), cout_ref[...])

    u = jax.nn.gelu(uv[:, :SGU_DIM])
    v = _layer_norm(jax.nn.gelu(uv[:, SGU_DIM:]), slg_ref[...], slb_ref[...])
    low = _low_half_lanes()
    mixed = []
    for c in range(tm // SGU_CHUNK):
        cols = []
        for j in range(SGU_DIM // LANES):
            vj = v[c * SGU_CHUNK:(c + 1) * SGU_CHUNK, j * LANES:(j + 1) * LANES]
            stacked = jnp.concatenate(
                [jnp.where(low, vj, 0.0), jnp.where(low, 0.0, vj)], axis=0).astype(BF16)
            cols.append(_dot(sw_ref[j], stacked))
        mixed.append(jnp.concatenate(cols, axis=1) + sb_ref[...])
    mixed = jnp.concatenate(mixed, axis=0) if len(mixed) > 1 else mixed[0]
    y_sgu = _dot((u * mixed).astype(BF16), sout_ref[...])
    m = (jax.nn.sigmoid(gates[0]) * y_conv + jax.nn.sigmoid(gates[1]) * y_attn
         + jax.nn.sigmoid(gates[2]) * y_sgu)

    o_ref[0] = x + mod_ref[0, 2:3, :] * _dot(m.astype(BF16), wo_ref[...])


def _merge(x, mod, g, y, attn, w4, b4, cw, cb, clg, clb, cout, aout,
           slg, slb, sw, sb, sout, wo):
    b, l, d = x.shape
    tm = min(TM_MERGE, l)
    hb = tm // CONV_HALO
    nhb = l // CONV_HALO
    weights = (w4, b4, cw, cb, clg, clb, cout, aout, slg, slb, sw, sb, sout, wo)
    return pl.pallas_call(
        functools.partial(_merge_kernel, tm=tm, rows=CONV_ROWS),
        grid=(b, l // tm),
        in_specs=[
            pl.BlockSpec((1, tm, d), lambda bi, i: (bi, i, 0)),
            pl.BlockSpec((1, 6, d), lambda bi, i: (bi, 0, 0)),
            _full_spec(g),
            pl.BlockSpec((1, tm, CONV_DIM), lambda bi, i: (bi, i, 0)),
            pl.BlockSpec((1, CONV_HALO, CONV_DIM),
                         lambda bi, i: (bi, jnp.maximum(i * hb - 1, 0), 0)),
            pl.BlockSpec((1, CONV_HALO, CONV_DIM),
                         lambda bi, i: (bi, jnp.minimum((i + 1) * hb, nhb - 1), 0)),
            pl.BlockSpec((1, tm, attn.shape[-1]), lambda bi, i: (bi, i, 0)),
        ] + [_full_spec(a) for a in weights],
        out_specs=pl.BlockSpec((1, tm, d), lambda bi, i: (bi, i, 0)),
        out_shape=jax.ShapeDtypeStruct((b, l, d), F32),
        scratch_shapes=[
            pltpu.VMEM((tm, d), BF16),
            pltpu.VMEM((SUBLANES, tm + 2 * CONV_HALO, CONV_DIM), F32),
            pltpu.VMEM((tm, CONV_DIM), F32),
        ],
        compiler_params=_cparams(2),
        name="mixer_merge",
    )(x, mod, g, y, y, y, attn, *weights)


def _ffn_kernel(x_ref, xp_ref, xn_ref, mod_ref, g_ref, up_ref, dw_ref, db_ref, down_ref,
                o_ref, h_ref, z_ref, act_ref, *, tm, ffn, rows):
    i = pl.program_id(1)
    n = pl.num_programs(1)
    g = g_ref[...]
    shift = mod_ref[0, 3:4, :]
    scale = mod_ref[0, 4:5, :]
    x = x_ref[0]
    h_ref[...] = jnp.concatenate(
        [jnp.where(i > 0, _norm_mod(xp_ref[0], g, shift, scale), 0.0),
         _norm_mod(x, g, shift, scale),
         jnp.where(i < n - 1, _norm_mod(xn_ref[0], g, shift, scale), 0.0)], axis=0).astype(BF16)
    ck = FFN_CHUNK
    n_chunks = ffn // ck
    ext = rows + 2 * FFN_HALO

    def offsets(c):
        return (c * ck, ffn + c * ck)

    def up_proj(c):
        for half, off in enumerate(offsets(c)):
            z_ref[c % 2, half] = _dot(h_ref[...], up_ref[:, off:off + ck])

    def conv_gate(c):
        for r0 in range(0, tm, rows):
            halves = []
            for half, off in enumerate(offsets(c)):
                z = z_ref[c % 2, half, r0:r0 + ext, :]
                halves.append(
                    db_ref[:, off:off + ck]
                    + dw_ref[0:1, off:off + ck] * pltpu.roll(z, 1, 0)[FFN_HALO:FFN_HALO + rows]
                    + dw_ref[1:2, off:off + ck] * z[FFN_HALO:FFN_HALO + rows]
                    + dw_ref[2:3, off:off + ck] * pltpu.roll(z, ext - 1, 0)[FFN_HALO:FFN_HALO + rows])
            a, gate = halves
            act_ref[r0:r0 + rows, c * ck:(c + 1) * ck] = (a * jax.nn.sigmoid(a) * gate).astype(BF16)

    up_proj(0)
    for c in range(n_chunks):
        if c + 1 < n_chunks:
            up_proj(c + 1)
        conv_gate(c)
    o_ref[0] = x + mod_ref[0, 5:6, :] * _dot(act_ref[...], down_ref[...])


def _conv_ffn(x, mod, g, up, dw, db, down):
    b, l, d = x.shape
    tm = min(TM_FFN, l)
    ffn = down.shape[0]
    hb = tm // FFN_HALO
    nhb = l // FFN_HALO
    return pl.pallas_call(
        functools.partial(_ffn_kernel, tm=tm, ffn=ffn, rows=min(FFN_ROWS, tm)),
        grid=(b, l // tm),
        in_specs=[
            pl.BlockSpec((1, tm, d), lambda bi, i: (bi, i, 0)),
            pl.BlockSpec((1, FFN_HALO, d), lambda bi, i: (bi, jnp.maximum(i * hb - 1, 0), 0)),
            pl.BlockSpec((1, FFN_HALO, d),
                         lambda bi, i: (bi, jnp.minimum((i + 1) * hb, nhb - 1), 0)),
            pl.BlockSpec((1, 6, d), lambda bi, i: (bi, 0, 0)),
            _full_spec(g), _full_spec(up), _full_spec(dw), _full_spec(db), _full_spec(down),
        ],
        out_specs=pl.BlockSpec((1, tm, d), lambda bi, i: (bi, i, 0)),
        out_shape=jax.ShapeDtypeStruct((b, l, d), F32),
        scratch_shapes=[
            pltpu.VMEM((tm + 2 * FFN_HALO, d), BF16),
            pltpu.VMEM((2, 2, tm + 2 * FFN_HALO, FFN_CHUNK), F32),
            pltpu.VMEM((tm, ffn), BF16),
        ],
        compiler_params=_cparams(2),
        name="conv_ffn",
    )(x, x, x, mod, g, up, dw, db, down)


def _rope_tables(seq):
    quarter = HEAD_DIM // 4
    inv = jnp.power(ROPE_BASE, -jnp.arange(quarter, dtype=F32) / quarter)
    t = jnp.arange(seq)
    rows = (t // GRID_W).astype(F32)[:, None] * inv[None, :]
    cols = (t % GRID_W).astype(F32)[:, None] * inv[None, :]
    cos = jnp.concatenate([jnp.cos(rows)] * 2 + [jnp.cos(cols)] * 2, axis=1)
    sin = jnp.concatenate([-jnp.sin(rows), jnp.sin(rows), -jnp.sin(cols), jnp.sin(cols)], axis=1)
    reps = LANES // HEAD_DIM
    return jnp.tile(cos, (1, reps)), jnp.tile(sin, (1, reps))


def kernel(x, c, ctx, c_ctx, ada_w, ada_b, norm1_g, norm2_g, w_in, gate_b, conv_dw_w, conv_dw_b, conv_ln_g, conv_ln_b, conv_out, q_norm_g, k_norm_g, attn_sink, attn_out, sgu_ln_g, sgu_ln_b, sgu_w, sgu_b, sgu_out, w_o, ffn_up, ffn_dw_w, ffn_dw_b, ffn_down):
    bsz, seq, d = x.shape
    depth = ada_w.shape[0]
    n_ctx = ctx.shape[1]

    n_rows = -(-(bsz + 1) // SUBLANES) * SUBLANES
    cond = jnp.zeros((n_rows, d), F32).at[:bsz].set(c).at[bsz].set(c_ctx)
    mods = _modulation(cond, ada_w, ada_b).reshape(depth, n_rows, 6, d)

    cos, sin = _rope_tables(seq)
    no_rope = jnp.zeros((n_ctx, LANES), F32)
    head_id = jnp.arange(LANES) // HEAD_DIM
    e = (head_id[:, None] == head_id[None, :]).astype(BF16)
    e = jnp.concatenate([e, e], axis=0)
    nab = 2 * CONV_DIM
    n_in1 = nab + (N_HEADS + 2 * N_KV_HEADS) * HEAD_DIM
    gc = SGU_DIM // SGU_GROUPS

    for l in range(depth):
        last = l == depth - 1
        mod_x = mods[l, :bsz]
        mod_c = jnp.broadcast_to(mods[l, bsz][None], (bsz, 6, d))
        g1 = norm1_g[l][None]
        g2 = norm2_g[l][None]
        w1 = w_in[l][:, :n_in1].astype(BF16)
        n_pre = 1 + N_BRANCH
        w4 = w_in[l][:, n_in1:].astype(BF16).reshape(d, n_pre, d).transpose(1, 0, 2)
        b4 = jnp.concatenate([jnp.zeros((1, d), F32), gate_b[l]], axis=0)[:, None, :]
        gqk = jnp.concatenate([jnp.tile(q_norm_g[l] * (LOG2E * HEAD_DIM ** -0.5), N_HEADS),
                               jnp.tile(k_norm_g[l], N_KV_HEADS)])[None]
        sw = sgu_w[l].reshape(SGU_GROUPS // 2, 2, SGU_CHUNK, SGU_CHUNK)
        sw = jnp.concatenate([sw[:, 0], sw[:, 1]], axis=-1).astype(BF16)
        sb = jnp.repeat(sgu_b[l].T, gc, axis=1)
        merge_w = (w4, b4,
                   conv_dw_w[l], conv_dw_b[l][None], conv_ln_g[l][None], conv_ln_b[l][None],
                   conv_out[l].astype(BF16), attn_out[l].astype(BF16),
                   sgu_ln_g[l][None], sgu_ln_b[l][None], sw, sb,
                   sgu_out[l].astype(BF16), w_o[l].astype(BF16))
        ffn_w = (ffn_up[l].astype(BF16), ffn_dw_w[l], ffn_dw_b[l][None], ffn_down[l].astype(BF16))
        sink = attn_sink[l] * LOG2E

        yc, qc, kc, vc = _in_proj(ctx, mod_c, g1, w1, e, gqk, no_rope, no_rope, rope=False)
        yx, qx, kx, vx = _in_proj(x, mod_x, g1, w1, e, gqk, cos, sin, rope=True)
        ax = _attention(sink, qx, kc, vc, kx, vx)
        x_mid = _merge(x, mod_x, g1, yx, ax, *merge_w)
        x_new = _conv_ffn(x_mid, mod_x, g2, *ffn_w)
        if not last:
            ac = _attention(sink, qc, kc, vc)
            c_mid = _merge(ctx, mod_c, g1, yc, ac, *merge_w)
            ctx = _conv_ffn(c_mid, mod_c, g2, *ffn_w)
        x = x_new
    return x
```

```python
import functools

import jax
import jax.numpy as jnp
from jax import lax
from jax.experimental import pallas as pl
from jax.experimental.pallas import tpu as pltpu

F32 = jnp.float32
BF16 = jnp.bfloat16

EPS = 1e-6
NEG_INF = -1e30
ROPE_BASE = 10000.0
LOG2E = 1.4426950408889634
GRID_W = 64
HEAD_DIM = 64
N_HEADS = 8
N_KV_HEADS = 2
WINDOW = 128
CONV_DIM = 512
CONV_WIDTH = 31
SGU_DIM = 512
SGU_GROUPS = 8
SGU_CHUNK = 128
FFN_CONV_WIDTH = 3
N_BRANCH = 3

LANES = 128
SUBLANES = 8
CONV_HALO = 16
FFN_HALO = 8
FFN_CHUNK = 256
FFN_ROWS = 256
CONV_ROWS = 64
VMEM_LIMIT = 56 * 1024 * 1024

TM_IN = 512
TQ_ATTN = 256
TM_MERGE = 256
TM_FFN = 512


def _cparams(n_axes):
    return pltpu.CompilerParams(
        dimension_semantics=("parallel",) * n_axes, vmem_limit_bytes=VMEM_LIMIT)


def _dot(a, b):
    return jnp.dot(a, b, preferred_element_type=F32)


def _dot_nt(a, b):
    return lax.dot_general(a, b, (((1,), (1,)), ((), ())), preferred_element_type=F32)


def _norm_mod(x, g, shift, scale):
    ms = jnp.mean(x * x, axis=-1, keepdims=True)
    y = x * lax.rsqrt(ms + EPS) * g
    return y * (1.0 + scale) + shift


def _layer_norm(x, g, b):
    mu = jnp.mean(x, axis=-1, keepdims=True)
    xc = x - mu
    var = jnp.mean(xc * xc, axis=-1, keepdims=True)
    return xc * lax.rsqrt(var + EPS) * g + b


def _low_half_lanes():
    return lax.broadcasted_iota(jnp.int32, (1, LANES), 1) < (LANES // 2)


def _full_spec(a):
    return pl.BlockSpec(a.shape, lambda bi, i: (0,) * a.ndim)


def _layer_spec(a, layer):
    return pl.BlockSpec((None,) + a.shape[1:], lambda bi, i: (layer,) + (0,) * (a.ndim - 1))


def _mod_kernel(c_ref, w_ref, b_ref, o_ref):
    c = c_ref[...]
    s = c * jax.nn.sigmoid(c)
    o_ref[0] = jnp.dot(s, w_ref[0], precision=lax.Precision.HIGHEST,
                       preferred_element_type=F32) + b_ref[0]


def _modulation(cond, ada_w, ada_b):
    depth, d, n6 = ada_w.shape
    r = cond.shape[0]
    tn = 1536
    return pl.pallas_call(
        _mod_kernel,
        grid=(depth, n6 // tn),
        in_specs=[
            pl.BlockSpec((r, d), lambda l, j: (0, 0)),
            pl.BlockSpec((1, d, tn), lambda l, j: (l, 0, j)),
            pl.BlockSpec((1, 1, tn), lambda l, j: (l, 0, j)),
        ],
        out_specs=pl.BlockSpec((1, r, tn), lambda l, j: (l, 0, j)),
        out_shape=jax.ShapeDtypeStruct((depth, r, n6), F32),
        compiler_params=_cparams(2),
        name="modulation",
    )(cond, ada_w, ada_b.reshape(depth, 1, n6))


def _in_kernel(x_ref, mod_ref, g_ref, w_ref, e_ref, gqk_ref, cos_ref, sin_ref,
               q_ref, k_ref, v_ref, *, rope):
    x = x_ref[0]
    h = _norm_mod(x, g_ref[...], mod_ref[0, 0:1, :], mod_ref[0, 1:2, :]).astype(BF16)
    qkv = _dot(h, w_ref[...])
    nq = N_HEADS * HEAD_DIM
    nqk = nq + N_KV_HEADS * HEAD_DIM
    e = e_ref[...]
    if rope:
        lane = lax.broadcasted_iota(jnp.int32, (1, LANES), 1)
        first_half = (lane % (HEAD_DIM // 2)) < (HEAD_DIM // 4)
        cos = cos_ref[...]
        sin = sin_ref[...]
    outs = []
    for j in range(nqk // LANES):
        t = qkv[:, j * LANES:(j + 1) * LANES]
        sq = t * t
        hi = sq.astype(BF16)
        lo = (sq - hi.astype(F32)).astype(BF16)
        ss = _dot(jnp.concatenate([hi, lo], axis=1), e)
        tn = t * lax.rsqrt(ss * (1.0 / HEAD_DIM) + EPS) * gqk_ref[:, j * LANES:(j + 1) * LANES]
        if rope:
            quarter = HEAD_DIM // 4
            partner = jnp.where(first_half,
                                pltpu.roll(tn, LANES - quarter, 1),
                                pltpu.roll(tn, quarter, 1))
            tn = tn * cos + partner * sin
        outs.append(tn)
    q_ref[0] = jnp.concatenate(outs[:nq // LANES], axis=1).astype(BF16)
    low = _low_half_lanes()
    kk = outs[nq // LANES]
    kr = pltpu.roll(kk, LANES // 2, 1)
    k_ref[0] = jnp.concatenate([jnp.where(low, kk, kr), jnp.where(low, kr, kk)], axis=1).astype(BF16)
    vv = qkv[:, nqk:]
    vr = pltpu.roll(vv, LANES // 2, 1)
    v_ref[0] = jnp.concatenate([jnp.where(low, vv, 0.0), jnp.where(low, 0.0, vr),
                                jnp.where(low, vr, 0.0), jnp.where(low, 0.0, vv)], axis=1).astype(BF16)


def _in_proj(x, mod, g, w, e, gqk, cos, sin, *, rope, layer):
    b, l, d = x.shape
    tm = min(TM_IN, l)
    nq = N_HEADS * HEAD_DIM
    nk = 2 * N_KV_HEADS * HEAD_DIM
    nv = 4 * N_KV_HEADS * HEAD_DIM
    return pl.pallas_call(
        functools.partial(_in_kernel, rope=rope),
        grid=(b, l // tm),
        in_specs=[
            pl.BlockSpec((1, tm, d), lambda bi, i: (bi, i, 0)),
            pl.BlockSpec((1, 6, d), lambda bi, i: (bi, 0, 0)),
            _layer_spec(g, layer), _layer_spec(w, layer), _full_spec(e), _layer_spec(gqk, layer),
            pl.BlockSpec((tm, LANES), lambda bi, i: (i, 0)),
            pl.BlockSpec((tm, LANES), lambda bi, i: (i, 0)),
        ],
        out_specs=[
            pl.BlockSpec((1, tm, nq), lambda bi, i: (bi, i, 0)),
            pl.BlockSpec((1, tm, nk), lambda bi, i: (bi, i, 0)),
            pl.BlockSpec((1, tm, nv), lambda bi, i: (bi, i, 0)),
        ],
        out_shape=[
            jax.ShapeDtypeStruct((b, l, nq), BF16),
            jax.ShapeDtypeStruct((b, l, nk), BF16),
            jax.ShapeDtypeStruct((b, l, nv), BF16),
        ],
        compiler_params=_cparams(2),
        name="in_proj_rope" if rope else "in_proj_ctx",
    )(x, mod, g, w, e, gqk, cos, sin)


def _attn_kernel(sink_ref, q_ref, kc_ref, vc_ref, *rest, local, tq, seq, layer):
    if local:
        k_ref, v_ref, o_ref, bias_ref = rest
        i = pl.program_id(1)
        width = tq + 2 * WINDOW
        t0 = i * tq
        start = pl.multiple_of(jnp.clip(t0 - WINDOW, 0, seq - width), WINDOW)
        qpos = t0 + lax.broadcasted_iota(jnp.int32, (tq, 1), 0)
        kpos = start + lax.broadcasted_iota(jnp.int32, (1, width), 1)
        bias_ref[...] = jnp.where(jnp.abs(qpos - kpos) <= WINDOW, 0.0, NEG_INF)
    else:
        (o_ref,) = rest
    low = _low_half_lanes()
    grp = N_HEADS // N_KV_HEADS
    n_ctx = kc_ref.shape[1]

    def scores(h):
        hk = h // grp
        qp = q_ref[0, :, (h // 2) * LANES:(h // 2 + 1) * LANES]
        qh = jnp.where(low, qp, 0.0) if h % 2 == 0 else jnp.where(low, 0.0, qp)
        s = _dot_nt(qh, kc_ref[0, :, hk * LANES:(hk + 1) * LANES])
        if local:
            s_loc = _dot_nt(qh, k_ref[0, pl.ds(start, width), hk * LANES:(hk + 1) * LANES])
            s = jnp.concatenate([s, s_loc + bias_ref[...]], axis=1)
        return s

    def softmax(h, s):
        sink = sink_ref[layer, h]
        m = jnp.maximum(jnp.max(s, axis=-1, keepdims=True), sink)
        p = jnp.exp2(s - m)
        den = jnp.sum(p, axis=-1, keepdims=True) + jnp.exp2(sink - m)
        return p.astype(BF16), 1.0 / den

    def values(h, p, inv):
        vs = slice((2 * (h // grp) + h % 2) * LANES, (2 * (h // grp) + h % 2 + 1) * LANES)
        o = _dot(p[:, :n_ctx], vc_ref[0, :, vs])
        if local:
            o = o + _dot(p[:, n_ctx:], v_ref[0, pl.ds(start, width), vs])
        return o * inv

    s = scores(0)
    o_pair = None
    for h in range(N_HEADS):
        s_next = scores(h + 1) if h + 1 < N_HEADS else None
        p, inv = softmax(h, s)
        o = values(h, p, inv)
        if h % 2 == 0:
            o_pair = o
        else:
            o_ref[0, :, (h // 2) * LANES:(h // 2 + 1) * LANES] = (o_pair + o).astype(o_ref.dtype)
        s = s_next


def _attention(sink, q, kc, vc, k=None, v=None, *, layer):
    b, l, nq = q.shape
    lc = kc.shape[1]
    local = k is not None
    tq = min(TQ_ATTN, l)
    if local:
        assert l >= tq + 2 * WINDOW and tq % WINDOW == 0
    in_specs = [
        pl.BlockSpec(memory_space=pltpu.SMEM),
        pl.BlockSpec((1, tq, nq), lambda bi, i: (bi, i, 0)),
        pl.BlockSpec((1, lc, kc.shape[2]), lambda bi, i: (bi, 0, 0)),
        pl.BlockSpec((1, lc, vc.shape[2]), lambda bi, i: (bi, 0, 0)),
    ]
    args = [sink, q, kc, vc]
    if local:
        in_specs += [pl.BlockSpec((1, l, k.shape[2]), lambda bi, i: (bi, 0, 0)),
                     pl.BlockSpec((1, l, v.shape[2]), lambda bi, i: (bi, 0, 0))]
        args += [k, v]
    return pl.pallas_call(
        functools.partial(_attn_kernel, local=local, tq=tq, seq=l, layer=layer),
        grid=(b, l // tq),
        in_specs=in_specs,
        out_specs=pl.BlockSpec((1, tq, nq), lambda bi, i: (bi, i, 0)),
        out_shape=jax.ShapeDtypeStruct((b, l, nq), BF16),
        scratch_shapes=[pltpu.VMEM((tq, tq + 2 * WINDOW), F32)] if local else [],
        compiler_params=_cparams(2),
        name="attn_window" if local else "attn_ctx",
    )(*args)


def _merge_kernel(x_ref, xp_ref, xn_ref, mod_ref, g_ref, a_ref,
                  wab_ref, w4_ref, b4_ref,
                  cw_ref, cb_ref, clg_ref, clb_ref, cout_ref, aout_ref,
                  slg_ref, slb_ref, sw_ref, sb_ref, sout_ref, wo_ref,
                  o_ref, h_ref, ys_ref, yc_ref, *, tm, rows):
    i = pl.program_id(1)
    n = pl.num_programs(1)
    g = g_ref[...]
    shift = mod_ref[0, 0:1, :]
    scale = mod_ref[0, 1:2, :]
    x = x_ref[0]
    h_ref[...] = jnp.concatenate(
        [jnp.where(i > 0, _norm_mod(xp_ref[0], g, shift, scale), 0.0),
         _norm_mod(x, g, shift, scale),
         jnp.where(i < n - 1, _norm_mod(xn_ref[0], g, shift, scale), 0.0)], axis=0).astype(BF16)
    ext = tm + 2 * CONV_HALO
    base = CONV_HALO - CONV_WIDTH // 2
    span = (base + CONV_WIDTH - 1) // SUBLANES * SUBLANES
    n_groups = CONV_DIM // LANES

    def glu_proj(c):
        ab = _dot(h_ref[...], wab_ref[:, 2 * c * LANES:2 * (c + 1) * LANES])
        y = ab[:, :LANES] * jax.nn.sigmoid(ab[:, LANES:])
        ys_ref[c % 2, 0] = y
        for r in range(1, SUBLANES):
            ys_ref[c % 2, r] = pltpu.roll(y, ext - r, 0)

    def taps(c):
        cols = slice(c * LANES, (c + 1) * LANES)
        for r0 in range(0, tm, rows):
            acc = jnp.broadcast_to(cb_ref[:, cols], (rows, LANES))
            for r in range(SUBLANES):
                slab = ys_ref[c % 2, r, r0:r0 + rows + span, :]
                for t in range(CONV_WIDTH):
                    if (base + t) % SUBLANES == r:
                        lo = base + t - r
                        acc = acc + cw_ref[t:t + 1, cols] * slab[lo:lo + rows]
            yc_ref[r0:r0 + rows, cols] = acc

    def h_main():
        return h_ref[CONV_HALO:CONV_HALO + tm, :]

    n_pre = w4_ref.shape[0]
    pre = []
    glu_proj(0)
    for c in range(n_groups):
        if c + 1 < n_groups:
            glu_proj(c + 1)
        if len(pre) < n_pre:
            pre.append(_dot(h_main(), w4_ref[len(pre)]) + b4_ref[len(pre)])
        taps(c)
    while len(pre) < n_pre:
        pre.append(_dot(h_main(), w4_ref[len(pre)]) + b4_ref[len(pre)])
    uv, gates = pre[0], pre[1:]
    y_attn = _dot(a_ref[0], aout_ref[...])

    yc = _layer_norm(yc_ref[...], clg_ref[...], clb_ref[...])
    yc = yc * jax.nn.sigmoid(yc)
    y_conv = _dot(yc.astype(BF16), cout_ref[...])

    u = jax.nn.gelu(uv[:, :SGU_DIM])
    v = _layer_norm(jax.nn.gelu(uv[:, SGU_DIM:]), slg_ref[...], slb_ref[...])
    low = _low_half_lanes()
    mixed = []
    for c in range(tm // SGU_CHUNK):
        cols = []
        for j in range(SGU_DIM // LANES):
            vj = v[c * SGU_CHUNK:(c + 1) * SGU_CHUNK, j * LANES:(j + 1) * LANES]
            stacked = jnp.concatenate(
                [jnp.where(low, vj, 0.0), jnp.where(low, 0.0, vj)], axis=0).astype(BF16)
            cols.append(_dot(sw_ref[j], stacked))
        mixed.append(jnp.concatenate(cols, axis=1) + sb_ref[...])
    mixed = jnp.concatenate(mixed, axis=0) if len(mixed) > 1 else mixed[0]
    y_sgu = _dot((u * mixed).astype(BF16), sout_ref[...])
    m = (jax.nn.sigmoid(gates[0]) * y_conv + jax.nn.sigmoid(gates[1]) * y_attn
         + jax.nn.sigmoid(gates[2]) * y_sgu)

    o_ref[0] = x + mod_ref[0, 2:3, :] * _dot(m.astype(BF16), wo_ref[...])


def _merge(x, mod, g, attn, wab, w4, b4, cw, cb, clg, clb, cout, aout,
           slg, slb, sw, sb, sout, wo, *, layer):
    b, l, d = x.shape
    tm = min(TM_MERGE, l)
    hb = tm // CONV_HALO
    nhb = l // CONV_HALO
    weights = (wab, w4, b4, cw, cb, clg, clb, cout, aout, slg, slb, sw, sb, sout, wo)
    return pl.pallas_call(
        functools.partial(_merge_kernel, tm=tm, rows=CONV_ROWS),
        grid=(b, l // tm),
        in_specs=[
            pl.BlockSpec((1, tm, d), lambda bi, i: (bi, i, 0)),
            pl.BlockSpec((1, CONV_HALO, d), lambda bi, i: (bi, jnp.maximum(i * hb - 1, 0), 0)),
            pl.BlockSpec((1, CONV_HALO, d),
                         lambda bi, i: (bi, jnp.minimum((i + 1) * hb, nhb - 1), 0)),
            pl.BlockSpec((1, 6, d), lambda bi, i: (bi, 0, 0)),
            _layer_spec(g, layer),
            pl.BlockSpec((1, tm, attn.shape[-1]), lambda bi, i: (bi, i, 0)),
        ] + [_layer_spec(a, layer) for a in weights],
        out_specs=pl.BlockSpec((1, tm, d), lambda bi, i: (bi, i, 0)),
        out_shape=jax.ShapeDtypeStruct((b, l, d), F32),
        scratch_shapes=[
            pltpu.VMEM((tm + 2 * CONV_HALO, d), BF16),
            pltpu.VMEM((2, SUBLANES, tm + 2 * CONV_HALO, LANES), F32),
            pltpu.VMEM((tm, CONV_DIM), F32),
        ],
        compiler_params=_cparams(2),
        name="mixer_merge",
    )(x, x, x, mod, g, attn, *weights)


def _ffn_kernel(x_ref, xp_ref, xn_ref, mod_ref, g_ref, up_ref, dw_ref, db_ref, down_ref,
                o_ref, h_ref, z_ref, act_ref, *, tm, ffn, rows):
    i = pl.program_id(1)
    n = pl.num_programs(1)
    g = g_ref[...]
    shift = mod_ref[0, 3:4, :]
    scale = mod_ref[0, 4:5, :]
    x = x_ref[0]
    h_ref[...] = jnp.concatenate(
        [jnp.where(i > 0, _norm_mod(xp_ref[0], g, shift, scale), 0.0),
         _norm_mod(x, g, shift, scale),
         jnp.where(i < n - 1, _norm_mod(xn_ref[0], g, shift, scale), 0.0)], axis=0).astype(BF16)
    ck = FFN_CHUNK
    n_chunks = ffn // ck
    ext = rows + 2 * FFN_HALO

    def offsets(c):
        return (c * ck, ffn + c * ck)

    def up_proj(c):
        for half, off in enumerate(offsets(c)):
            z_ref[c % 2, half] = _dot(h_ref[...], up_ref[:, off:off + ck])

    def conv_gate(c):
        for r0 in range(0, tm, rows):
            halves = []
            for half, off in enumerate(offsets(c)):
                z = z_ref[c % 2, half, r0:r0 + ext, :]
                halves.append(
                    db_ref[:, off:off + ck]
                    + dw_ref[0:1, off:off + ck] * pltpu.roll(z, 1, 0)[FFN_HALO:FFN_HALO + rows]
                    + dw_ref[1:2, off:off + ck] * z[FFN_HALO:FFN_HALO + rows]
                    + dw_ref[2:3, off:off + ck] * pltpu.roll(z, ext - 1, 0)[FFN_HALO:FFN_HALO + rows])
            a, gate = halves
            act_ref[r0:r0 + rows, c * ck:(c + 1) * ck] = (a * jax.nn.sigmoid(a) * gate).astype(BF16)

    up_proj(0)
    for c in range(n_chunks):
        if c + 1 < n_chunks:
            up_proj(c + 1)
        conv_gate(c)
    o_ref[0] = x + mod_ref[0, 5:6, :] * _dot(act_ref[...], down_ref[...])


def _conv_ffn(x, mod, g, up, dw, db, down, *, layer):
    b, l, d = x.shape
    tm = min(TM_FFN, l)
    ffn = down.shape[1]
    hb = tm // FFN_HALO
    nhb = l // FFN_HALO
    return pl.pallas_call(
        functools.partial(_ffn_kernel, tm=tm, ffn=ffn, rows=min(FFN_ROWS, tm)),
        grid=(b, l // tm),
        in_specs=[
            pl.BlockSpec((1, tm, d), lambda bi, i: (bi, i, 0)),
            pl.BlockSpec((1, FFN_HALO, d), lambda bi, i: (bi, jnp.maximum(i * hb - 1, 0), 0)),
            pl.BlockSpec((1, FFN_HALO, d),
                         lambda bi, i: (bi, jnp.minimum((i + 1) * hb, nhb - 1), 0)),
            pl.BlockSpec((1, 6, d), lambda bi, i: (bi, 0, 0)),
            _layer_spec(g, layer), _layer_spec(up, layer), _layer_spec(dw, layer), _layer_spec(db, layer),
            _layer_spec(down, layer),
        ],
        out_specs=pl.BlockSpec((1, tm, d), lambda bi, i: (bi, i, 0)),
        out_shape=jax.ShapeDtypeStruct((b, l, d), F32),
        scratch_shapes=[
            pltpu.VMEM((tm + 2 * FFN_HALO, d), BF16),
            pltpu.VMEM((2, 2, tm + 2 * FFN_HALO, FFN_CHUNK), F32),
            pltpu.VMEM((tm, ffn), BF16),
        ],
        compiler_params=_cparams(2),
        name="conv_ffn",
    )(x, x, x, mod, g, up, dw, db, down)


def _rope_tables(seq):
    quarter = HEAD_DIM // 4
    inv = jnp.power(ROPE_BASE, -jnp.arange(quarter, dtype=F32) / quarter)
    t = jnp.arange(seq)
    rows = (t // GRID_W).astype(F32)[:, None] * inv[None, :]
    cols = (t % GRID_W).astype(F32)[:, None] * inv[None, :]
    cos = jnp.concatenate([jnp.cos(rows)] * 2 + [jnp.cos(cols)] * 2, axis=1)
    sin = jnp.concatenate([-jnp.sin(rows), jnp.sin(rows), -jnp.sin(cols), jnp.sin(cols)], axis=1)
    reps = LANES // HEAD_DIM
    return jnp.tile(cos, (1, reps)), jnp.tile(sin, (1, reps))


def kernel(x, c, ctx, c_ctx, ada_w, ada_b, norm1_g, norm2_g, w_in, gate_b, conv_dw_w, conv_dw_b, conv_ln_g, conv_ln_b, conv_out, q_norm_g, k_norm_g, attn_sink, attn_out, sgu_ln_g, sgu_ln_b, sgu_w, sgu_b, sgu_out, w_o, ffn_up, ffn_dw_w, ffn_dw_b, ffn_down):
    bsz, seq, d = x.shape
    depth = ada_w.shape[0]
    n_ctx = ctx.shape[1]

    n_rows = -(-(bsz + 1) // SUBLANES) * SUBLANES
    cond = jnp.zeros((n_rows, d), F32).at[:bsz].set(c).at[bsz].set(c_ctx)
    mods = _modulation(cond, ada_w, ada_b).reshape(depth, n_rows, 6, d)

    cos, sin = _rope_tables(seq)
    no_rope = jnp.zeros((n_ctx, LANES), F32)
    head_id = jnp.arange(LANES) // HEAD_DIM
    e = (head_id[:, None] == head_id[None, :]).astype(BF16)
    e = jnp.concatenate([e, e], axis=0)

    nab = 2 * CONV_DIM
    n_in1 = nab + (N_HEADS + 2 * N_KV_HEADS) * HEAD_DIM
    groups = CONV_DIM // LANES
    n_pre = 1 + N_BRANCH
    assert w_in.shape[2] - n_in1 == n_pre * d

    def row(p):
        return p[:, None, :]

    wqkv = w_in[:, :, nab:n_in1].astype(BF16)
    wab = (w_in[:, :, :nab].astype(BF16).reshape(depth, d, 2, groups, LANES)
           .transpose(0, 1, 3, 2, 4).reshape(depth, d, nab))
    w4 = w_in[:, :, n_in1:].astype(BF16).reshape(depth, d, n_pre, d).transpose(0, 2, 1, 3)
    b4 = jnp.concatenate([jnp.zeros((depth, 1, d), F32), gate_b], axis=1)[:, :, None, :]
    gqk = row(jnp.concatenate([jnp.tile(q_norm_g * (LOG2E * HEAD_DIM ** -0.5), (1, N_HEADS)),
                               jnp.tile(k_norm_g, (1, N_KV_HEADS))], axis=1))
    sink = attn_sink * LOG2E
    sw = sgu_w.reshape(depth, SGU_GROUPS // 2, 2, SGU_CHUNK, SGU_CHUNK)
    sw = jnp.concatenate([sw[:, :, 0], sw[:, :, 1]], axis=-1).astype(BF16)
    sb = jnp.repeat(jnp.swapaxes(sgu_b, 1, 2), SGU_DIM // SGU_GROUPS, axis=2)
    g1 = row(norm1_g)
    g2 = row(norm2_g)
    merge_w = (wab, w4, b4, conv_dw_w, row(conv_dw_b), row(conv_ln_g), row(conv_ln_b),
               conv_out.astype(BF16), attn_out.astype(BF16), row(sgu_ln_g), row(sgu_ln_b), sw, sb,
               sgu_out.astype(BF16), w_o.astype(BF16))
    ffn_w = (ffn_up.astype(BF16), ffn_dw_w, row(ffn_dw_b), ffn_down.astype(BF16))

    for l in range(depth):
        last = l == depth - 1
        mod_x = mods[l, :bsz]
        mod_c = jnp.broadcast_to(mods[l, bsz][None], (bsz, 6, d))
        qc, kc, vc = _in_proj(ctx, mod_c, g1, wqkv, e, gqk, no_rope, no_rope, rope=False, layer=l)
        qx, kx, vx = _in_proj(x, mod_x, g1, wqkv, e, gqk, cos, sin, rope=True, layer=l)
        ax = _attention(sink, qx, kc, vc, kx, vx, layer=l)
        x_mid = _merge(x, mod_x, g1, ax, *merge_w, layer=l)
        x_new = _conv_ffn(x_mid, mod_x, g2, *ffn_w, layer=l)
        if not last:
            ac = _attention(sink, qc, kc, vc, layer=l)
            c_mid = _merge(ctx, mod_c, g1, ac, *merge_w, layer=l)
            ctx = _conv_ffn(c_mid, mod_c, g2, *ffn_w, layer=l)
        x = x_new
    return x
```

```python
import functools

import jax
import jax.numpy as jnp
from jax import lax
from jax.experimental import pallas as pl
from jax.experimental.pallas import tpu as pltpu

F32 = jnp.float32
BF16 = jnp.bfloat16

EPS = 1e-6
NEG_INF = -1e30
ROPE_BASE = 10000.0
LOG2E = 1.4426950408889634
GRID_W = 64
HEAD_DIM = 64
N_HEADS = 8
N_KV_HEADS = 2
WINDOW = 128
CONV_DIM = 512
CONV_WIDTH = 31
SGU_DIM = 512
SGU_GROUPS = 8
SGU_CHUNK = 128
FFN_CONV_WIDTH = 3
N_BRANCH = 3

LANES = 128
SUBLANES = 8
CONV_HALO = 16
FFN_HALO = 8
FFN_CHUNK = 256
FFN_ROWS = 256
CONV_ROWS = 64
VMEM_LIMIT = 56 * 1024 * 1024

TM_IN = 512
TQ_ATTN = 256
TM_MERGE = 512
TM_FFN = 1024


def _cparams(n_axes):
    return pltpu.CompilerParams(
        dimension_semantics=("parallel",) * n_axes, vmem_limit_bytes=VMEM_LIMIT)


def _dot(a, b):
    return jnp.dot(a, b, preferred_element_type=F32)


def _dot_nt(a, b):
    return lax.dot_general(a, b, (((1,), (1,)), ((), ())), preferred_element_type=F32)


def _norm_mod(x, g, shift, scale):
    ms = jnp.mean(x * x, axis=-1, keepdims=True)
    y = x * lax.rsqrt(ms + EPS) * g
    return y * (1.0 + scale) + shift


def _layer_norm(x, g, b):
    mu = jnp.mean(x, axis=-1, keepdims=True)
    xc = x - mu
    var = jnp.mean(xc * xc, axis=-1, keepdims=True)
    return xc * lax.rsqrt(var + EPS) * g + b


def _low_half_lanes():
    return lax.broadcasted_iota(jnp.int32, (1, LANES), 1) < (LANES // 2)


def _full_spec(a):
    return pl.BlockSpec(a.shape, lambda bi, i: (0,) * a.ndim)


def _layer_spec(a, layer):
    return pl.BlockSpec((None,) + a.shape[1:], lambda bi, i: (layer,) + (0,) * (a.ndim - 1),
                        pipeline_mode=pl.Buffered(1))


def _mod_kernel(c_ref, w_ref, b_ref, o_ref):
    c = c_ref[...]
    s = c * jax.nn.sigmoid(c)
    w = w_ref[0]
    s_hi = s.astype(BF16)
    s_lo = (s - s_hi.astype(F32)).astype(BF16)
    w_hi = w.astype(BF16)
    w_lo = (w - w_hi.astype(F32)).astype(BF16)
    o_ref[0] = _dot(s_hi, w_hi) + _dot(s_lo, w_hi) + _dot(s_hi, w_lo) + b_ref[0]


def _modulation(cond, ada_w, ada_b):
    depth, d, n6 = ada_w.shape
    r = cond.shape[0]
    tn = 1536
    return pl.pallas_call(
        _mod_kernel,
        grid=(depth, n6 // tn),
        in_specs=[
            pl.BlockSpec((r, d), lambda l, j: (0, 0)),
            pl.BlockSpec((1, d, tn), lambda l, j: (l, 0, j)),
            pl.BlockSpec((1, 1, tn), lambda l, j: (l, 0, j)),
        ],
        out_specs=pl.BlockSpec((1, r, tn), lambda l, j: (l, 0, j)),
        out_shape=jax.ShapeDtypeStruct((depth, r, n6), F32),
        compiler_params=_cparams(2),
        name="modulation",
    )(cond, ada_w, ada_b.reshape(depth, 1, n6))


def _in_kernel(x_ref, mod_ref, g_ref, w_ref, e_ref, gqk_ref, cos_ref, sin_ref,
               q_ref, k_ref, v_ref, *, rope):
    x = x_ref[0]
    h = _norm_mod(x, g_ref[...], mod_ref[0, 0:1, :], mod_ref[0, 1:2, :]).astype(BF16)
    nab = 2 * CONV_DIM
    qkv = _dot(h, w_ref[:, nab:nab + (N_HEADS + 2 * N_KV_HEADS) * HEAD_DIM])
    nq = N_HEADS * HEAD_DIM
    nqk = nq + N_KV_HEADS * HEAD_DIM
    e = e_ref[...]
    if rope:
        lane = lax.broadcasted_iota(jnp.int32, (1, LANES), 1)
        first_half = (lane % (HEAD_DIM // 2)) < (HEAD_DIM // 4)
        cos = cos_ref[...]
        sin = sin_ref[...]
    outs = []
    for j in range(nqk // LANES):
        t = qkv[:, j * LANES:(j + 1) * LANES]
        sq = t * t
        hi = sq.astype(BF16)
        lo = (sq - hi.astype(F32)).astype(BF16)
        ss = _dot(jnp.concatenate([hi, lo], axis=1), e)
        tn = t * lax.rsqrt(ss * (1.0 / HEAD_DIM) + EPS) * gqk_ref[:, j * LANES:(j + 1) * LANES]
        if rope:
            quarter = HEAD_DIM // 4
            partner = jnp.where(first_half,
                                pltpu.roll(tn, LANES - quarter, 1),
                                pltpu.roll(tn, quarter, 1))
            tn = tn * cos + partner * sin
        outs.append(tn)
    q_ref[0] = jnp.concatenate(outs[:nq // LANES], axis=1).astype(BF16)
    low = _low_half_lanes()
    kk = outs[nq // LANES]
    kr = pltpu.roll(kk, LANES // 2, 1)
    k_ref[0] = jnp.concatenate([jnp.where(low, kk, kr), jnp.where(low, kr, kk)], axis=1).astype(BF16)
    vv = qkv[:, nqk:]
    vr = pltpu.roll(vv, LANES // 2, 1)
    v_ref[0] = jnp.concatenate([jnp.where(low, vv, 0.0), jnp.where(low, 0.0, vr),
                                jnp.where(low, vr, 0.0), jnp.where(low, 0.0, vv)], axis=1).astype(BF16)


def _in_proj(x, mod, g, w, e, gqk, cos, sin, *, rope, layer):
    b, l, d = x.shape
    tm = min(TM_IN, l)
    nq = N_HEADS * HEAD_DIM
    nk = 2 * N_KV_HEADS * HEAD_DIM
    nv = 4 * N_KV_HEADS * HEAD_DIM
    return pl.pallas_call(
        functools.partial(_in_kernel, rope=rope),
        grid=(b, l // tm),
        in_specs=[
            pl.BlockSpec((1, tm, d), lambda bi, i: (bi, i, 0)),
            pl.BlockSpec((1, 6, d), lambda bi, i: (bi, 0, 0)),
            _layer_spec(g, layer), _layer_spec(w, layer), _full_spec(e), _layer_spec(gqk, layer),
            pl.BlockSpec((tm, LANES), lambda bi, i: (i, 0)),
            pl.BlockSpec((tm, LANES), lambda bi, i: (i, 0)),
        ],
        out_specs=[
            pl.BlockSpec((1, tm, nq), lambda bi, i: (bi, i, 0)),
            pl.BlockSpec((1, tm, nk), lambda bi, i: (bi, i, 0)),
            pl.BlockSpec((1, tm, nv), lambda bi, i: (bi, i, 0)),
        ],
        out_shape=[
            jax.ShapeDtypeStruct((b, l, nq), BF16),
            jax.ShapeDtypeStruct((b, l, nk), BF16),
            jax.ShapeDtypeStruct((b, l, nv), BF16),
        ],
        compiler_params=_cparams(2),
        name="in_proj_rope" if rope else "in_proj_ctx",
    )(x, mod, g, w, e, gqk, cos, sin)


def _attn_kernel(sink_ref, q_ref, kc_ref, vc_ref, *rest, local, tq, seq, layer):
    if local:
        k_ref, v_ref, o_ref, bias_ref = rest
        i = pl.program_id(1)
        width = tq + 2 * WINDOW
        t0 = i * tq
        start = pl.multiple_of(jnp.clip(t0 - WINDOW, 0, seq - width), WINDOW)
        qpos = t0 + lax.broadcasted_iota(jnp.int32, (tq, 1), 0)
        kpos = start + lax.broadcasted_iota(jnp.int32, (1, width), 1)
        bias_ref[...] = jnp.where(jnp.abs(qpos - kpos) <= WINDOW, 0.0, NEG_INF)
    else:
        (o_ref,) = rest
    low = _low_half_lanes()
    grp = N_HEADS // N_KV_HEADS
    n_ctx = kc_ref.shape[1]

    def scores(h):
        hk = h // grp
        qp = q_ref[0, :, (h // 2) * LANES:(h // 2 + 1) * LANES]
        qh = jnp.where(low, qp, 0.0) if h % 2 == 0 else jnp.where(low, 0.0, qp)
        s = _dot_nt(qh, kc_ref[0, :, hk * LANES:(hk + 1) * LANES])
        if local:
            s_loc = _dot_nt(qh, k_ref[0, pl.ds(start, width), hk * LANES:(hk + 1) * LANES])
            s = jnp.concatenate([s, s_loc + bias_ref[...]], axis=1)
        return s

    def softmax(h, s):
        sink = sink_ref[layer, h]
        m = jnp.maximum(jnp.max(s, axis=-1, keepdims=True), sink)
        p = jnp.exp2(s - m)
        den = jnp.sum(p, axis=-1, keepdims=True) + jnp.exp2(sink - m)
        return p.astype(BF16), 1.0 / den

    def values(h, p, inv):
        vs = slice((2 * (h // grp) + h % 2) * LANES, (2 * (h // grp) + h % 2 + 1) * LANES)
        o = _dot(p[:, :n_ctx], vc_ref[0, :, vs])
        if local:
            o = o + _dot(p[:, n_ctx:], v_ref[0, pl.ds(start, width), vs])
        return o * inv

    s = scores(0)
    o_pair = None
    for h in range(N_HEADS):
        s_next = scores(h + 1) if h + 1 < N_HEADS else None
        p, inv = softmax(h, s)
        o = values(h, p, inv)
        if h % 2 == 0:
            o_pair = o
        else:
            o_ref[0, :, (h // 2) * LANES:(h // 2 + 1) * LANES] = (o_pair + o).astype(o_ref.dtype)
        s = s_next


def _attention(sink, q, kc, vc, k=None, v=None, *, layer):
    b, l, nq = q.shape
    lc = kc.shape[1]
    local = k is not None
    tq = min(TQ_ATTN, l)
    if local:
        assert l >= tq + 2 * WINDOW and tq % WINDOW == 0
    in_specs = [
        pl.BlockSpec(memory_space=pltpu.SMEM),
        pl.BlockSpec((1, tq, nq), lambda bi, i: (bi, i, 0)),
        pl.BlockSpec((1, lc, kc.shape[2]), lambda bi, i: (bi, 0, 0)),
        pl.BlockSpec((1, lc, vc.shape[2]), lambda bi, i: (bi, 0, 0)),
    ]
    args = [sink, q, kc, vc]
    if local:
        in_specs += [pl.BlockSpec((1, l, k.shape[2]), lambda bi, i: (bi, 0, 0)),
                     pl.BlockSpec((1, l, v.shape[2]), lambda bi, i: (bi, 0, 0))]
        args += [k, v]
    return pl.pallas_call(
        functools.partial(_attn_kernel, local=local, tq=tq, seq=l, layer=layer),
        grid=(b, l // tq),
        in_specs=in_specs,
        out_specs=pl.BlockSpec((1, tq, nq), lambda bi, i: (bi, i, 0)),
        out_shape=jax.ShapeDtypeStruct((b, l, nq), BF16),
        scratch_shapes=[pltpu.VMEM((tq, tq + 2 * WINDOW), F32)] if local else [],
        compiler_params=_cparams(2),
        name="attn_window" if local else "attn_ctx",
    )(*args)


def _merge_kernel(x_ref, xp_ref, xn_ref, mod_ref, g_ref, a_ref,
                  wab_ref, win_ref, gb_ref,
                  cw_ref, cb_ref, clg_ref, clb_ref, cout_ref, aout_ref,
                  slg_ref, slb_ref, sw_ref, sb_ref, sout_ref, wo_ref,
                  o_ref, h_ref, ys_ref, yc_ref, *, tm, rows):
    i = pl.program_id(1)
    n = pl.num_programs(1)
    g = g_ref[...]
    shift = mod_ref[0, 0:1, :]
    scale = mod_ref[0, 1:2, :]
    x = x_ref[0]
    h_ref[...] = jnp.concatenate(
        [jnp.where(i > 0, _norm_mod(xp_ref[0], g, shift, scale), 0.0),
         _norm_mod(x, g, shift, scale),
         jnp.where(i < n - 1, _norm_mod(xn_ref[0], g, shift, scale), 0.0)], axis=0).astype(BF16)
    ext = tm + 2 * CONV_HALO
    base = CONV_HALO - CONV_WIDTH // 2
    span = (base + CONV_WIDTH - 1) // SUBLANES * SUBLANES
    n_groups = CONV_DIM // LANES

    def glu_proj(c):
        ab = _dot(h_ref[...], wab_ref[:, 2 * c * LANES:2 * (c + 1) * LANES])
        y = ab[:, :LANES] * jax.nn.sigmoid(ab[:, LANES:])
        ys_ref[c % 2, 0] = y
        for r in range(1, SUBLANES):
            ys_ref[c % 2, r] = pltpu.roll(y, ext - r, 0)

    def taps(c):
        cols = slice(c * LANES, (c + 1) * LANES)
        for r0 in range(0, tm, rows):
            acc = jnp.broadcast_to(cb_ref[:, cols], (rows, LANES))
            for r in range(SUBLANES):
                slab = ys_ref[c % 2, r, r0:r0 + rows + span, :]
                for t in range(CONV_WIDTH):
                    if (base + t) % SUBLANES == r:
                        lo = base + t - r
                        acc = acc + cw_ref[t:t + 1, cols] * slab[lo:lo + rows]
            yc_ref[r0:r0 + rows, cols] = acc

    def h_main():
        return h_ref[CONV_HALO:CONV_HALO + tm, :]

    d = x.shape[-1]
    rest = 2 * CONV_DIM + (N_HEADS + 2 * N_KV_HEADS) * HEAD_DIM

    def h_proj(k):
        out = _dot(h_main(), win_ref[:, rest + k * d:rest + (k + 1) * d])
        return out if k == 0 else out + gb_ref[k - 1]

    n_pre = 1 + N_BRANCH
    pre = []
    glu_proj(0)
    for c in range(n_groups):
        if c + 1 < n_groups:
            glu_proj(c + 1)
        if len(pre) < n_pre:
            pre.append(h_proj(len(pre)))
        taps(c)
    while len(pre) < n_pre:
        pre.append(h_proj(len(pre)))
    uv, gates = pre[0], pre[1:]
    y_attn = _dot(a_ref[0], aout_ref[...])

    yc = _layer_norm(yc_ref[...], clg_ref[...], clb_ref[...])
    yc = yc * jax.nn.sigmoid(yc)
    y_conv = _dot(yc.astype(BF16), cout_ref[...])

    u = jax.nn.gelu(uv[:, :SGU_DIM])
    v = _layer_norm(jax.nn.gelu(uv[:, SGU_DIM:]), slg_ref[...], slb_ref[...])
    low = _low_half_lanes()
    mixed = []
    for c in range(tm // SGU_CHUNK):
        cols = []
        for j in range(SGU_DIM // LANES):
            vj = v[c * SGU_CHUNK:(c + 1) * SGU_CHUNK, j * LANES:(j + 1) * LANES]
            stacked = jnp.concatenate(
                [jnp.where(low, vj, 0.0), jnp.where(low, 0.0, vj)], axis=0).astype(BF16)
            cols.append(_dot(sw_ref[j], stacked))
        mixed.append(jnp.concatenate(cols, axis=1) + sb_ref[...])
    mixed = jnp.concatenate(mixed, axis=0) if len(mixed) > 1 else mixed[0]
    y_sgu = _dot((u * mixed).astype(BF16), sout_ref[...])
    m = (jax.nn.sigmoid(gates[0]) * y_conv + jax.nn.sigmoid(gates[1]) * y_attn
         + jax.nn.sigmoid(gates[2]) * y_sgu)

    o_ref[0] = x + mod_ref[0, 2:3, :] * _dot(m.astype(BF16), wo_ref[...])


def _merge(x, mod, g, attn, wab, win, gb, cw, cb, clg, clb, cout, aout,
           slg, slb, sw, sb, sout, wo, *, layer):
    b, l, d = x.shape
    tm = min(TM_MERGE, l)
    hb = tm // CONV_HALO
    nhb = l // CONV_HALO
    weights = (wab, win, gb, cw, cb, clg, clb, cout, aout, slg, slb, sw, sb, sout, wo)
    return pl.pallas_call(
        functools.partial(_merge_kernel, tm=tm, rows=CONV_ROWS),
        grid=(b, l // tm),
        in_specs=[
            pl.BlockSpec((1, tm, d), lambda bi, i: (bi, i, 0)),
            pl.BlockSpec((1, CONV_HALO, d), lambda bi, i: (bi, jnp.maximum(i * hb - 1, 0), 0)),
            pl.BlockSpec((1, CONV_HALO, d),
                         lambda bi, i: (bi, jnp.minimum((i + 1) * hb, nhb - 1), 0)),
            pl.BlockSpec((1, 6, d), lambda bi, i: (bi, 0, 0)),
            _layer_spec(g, layer),
            pl.BlockSpec((1, tm, attn.shape[-1]), lambda bi, i: (bi, i, 0)),
        ] + [_layer_spec(a, layer) for a in weights],
        out_specs=pl.BlockSpec((1, tm, d), lambda bi, i: (bi, i, 0)),
        out_shape=jax.ShapeDtypeStruct((b, l, d), F32),
        scratch_shapes=[
            pltpu.VMEM((tm + 2 * CONV_HALO, d), BF16),
            pltpu.VMEM((2, SUBLANES, tm + 2 * CONV_HALO, LANES), F32),
            pltpu.VMEM((tm, CONV_DIM), F32),
        ],
        compiler_params=_cparams(2),
        name="mixer_merge",
    )(x, x, x, mod, g, attn, *weights)


def _ffn_kernel(x_ref, xp_ref, xn_ref, mod_ref, g_ref, up_ref, dw_ref, db_ref, down_ref,
                o_ref, h_ref, z_ref, act_ref, *, tm, ffn, rows):
    i = pl.program_id(1)
    n = pl.num_programs(1)
    g = g_ref[...]
    shift = mod_ref[0, 3:4, :]
    scale = mod_ref[0, 4:5, :]
    x = x_ref[0]
    h_ref[...] = jnp.concatenate(
        [jnp.where(i > 0, _norm_mod(xp_ref[0], g, shift, scale), 0.0),
         _norm_mod(x, g, shift, scale),
         jnp.where(i < n - 1, _norm_mod(xn_ref[0], g, shift, scale), 0.0)], axis=0).astype(BF16)
    ck = FFN_CHUNK
    n_chunks = ffn // ck
    ext = rows + 2 * FFN_HALO

    def offsets(c):
        return (c * ck, ffn + c * ck)

    def up_proj(c):
        for half, off in enumerate(offsets(c)):
            z_ref[c % 2, half] = _dot(h_ref[...], up_ref[:, off:off + ck])

    def conv_gate(c):
        for r0 in range(0, tm, rows):
            halves = []
            for half, off in enumerate(offsets(c)):
                z = z_ref[c % 2, half, r0:r0 + ext, :]
                halves.append(
                    db_ref[:, off:off + ck]
                    + dw_ref[0:1, off:off + ck] * pltpu.roll(z, 1, 0)[FFN_HALO:FFN_HALO + rows]
                    + dw_ref[1:2, off:off + ck] * z[FFN_HALO:FFN_HALO + rows]
                    + dw_ref[2:3, off:off + ck] * pltpu.roll(z, ext - 1, 0)[FFN_HALO:FFN_HALO + rows])
            a, gate = halves
            act_ref[r0:r0 + rows, c * ck:(c + 1) * ck] = (a * jax.nn.sigmoid(a) * gate).astype(BF16)

    up_proj(0)
    for c in range(n_chunks):
        if c + 1 < n_chunks:
            up_proj(c + 1)
        conv_gate(c)
    o_ref[0] = x + mod_ref[0, 5:6, :] * _dot(act_ref[...], down_ref[...])


def _conv_ffn(x, mod, g, up, dw, db, down, *, layer):
    b, l, d = x.shape
    tm = min(TM_FFN, l)
    ffn = down.shape[1]
    hb = tm // FFN_HALO
    nhb = l // FFN_HALO
    return pl.pallas_call(
        functools.partial(_ffn_kernel, tm=tm, ffn=ffn, rows=min(FFN_ROWS, tm)),
        grid=(b, l // tm),
        in_specs=[
            pl.BlockSpec((1, tm, d), lambda bi, i: (bi, i, 0)),
            pl.BlockSpec((1, FFN_HALO, d), lambda bi, i: (bi, jnp.maximum(i * hb - 1, 0), 0)),
            pl.BlockSpec((1, FFN_HALO, d),
                         lambda bi, i: (bi, jnp.minimum((i + 1) * hb, nhb - 1), 0)),
            pl.BlockSpec((1, 6, d), lambda bi, i: (bi, 0, 0)),
            _layer_spec(g, layer), _layer_spec(up, layer), _layer_spec(dw, layer), _layer_spec(db, layer),
            _layer_spec(down, layer),
        ],
        out_specs=pl.BlockSpec((1, tm, d), lambda bi, i: (bi, i, 0)),
        out_shape=jax.ShapeDtypeStruct((b, l, d), F32),
        scratch_shapes=[
            pltpu.VMEM((tm + 2 * FFN_HALO, d), BF16),
            pltpu.VMEM((2, 2, tm + 2 * FFN_HALO, FFN_CHUNK), F32),
            pltpu.VMEM((tm, ffn), BF16),
        ],
        compiler_params=_cparams(2),
        name="conv_ffn",
    )(x, x, x, mod, g, up, dw, db, down)


def _rope_tables(seq):
    quarter = HEAD_DIM // 4
    inv = jnp.power(ROPE_BASE, -jnp.arange(quarter, dtype=F32) / quarter)
    t = jnp.arange(seq)
    rows = (t // GRID_W).astype(F32)[:, None] * inv[None, :]
    cols = (t % GRID_W).astype(F32)[:, None] * inv[None, :]
    cos = jnp.concatenate([jnp.cos(rows)] * 2 + [jnp.cos(cols)] * 2, axis=1)
    sin = jnp.concatenate([-jnp.sin(rows), jnp.sin(rows), -jnp.sin(cols), jnp.sin(cols)], axis=1)
    reps = LANES // HEAD_DIM
    return jnp.tile(cos, (1, reps)), jnp.tile(sin, (1, reps))


def kernel(x, c, ctx, c_ctx, ada_w, ada_b, norm1_g, norm2_g, w_in, gate_b, conv_dw_w, conv_dw_b, conv_ln_g, conv_ln_b, conv_out, q_norm_g, k_norm_g, attn_sink, attn_out, sgu_ln_g, sgu_ln_b, sgu_w, sgu_b, sgu_out, w_o, ffn_up, ffn_dw_w, ffn_dw_b, ffn_down):
    bsz, seq, d = x.shape
    depth = ada_w.shape[0]
    n_ctx = ctx.shape[1]

    n_rows = -(-(bsz + 1) // SUBLANES) * SUBLANES
    cond = jnp.zeros((n_rows, d), F32).at[:bsz].set(c).at[bsz].set(c_ctx)
    mods = _modulation(cond, ada_w, ada_b).reshape(depth, n_rows, 6, d)

    cos, sin = _rope_tables(seq)
    no_rope = jnp.zeros((n_ctx, LANES), F32)
    head_id = jnp.arange(LANES) // HEAD_DIM
    e = (head_id[:, None] == head_id[None, :]).astype(BF16)
    e = jnp.concatenate([e, e], axis=0)

    nab = 2 * CONV_DIM
    n_in1 = nab + (N_HEADS + 2 * N_KV_HEADS) * HEAD_DIM
    groups = CONV_DIM // LANES
    assert w_in.shape[2] - n_in1 == (1 + N_BRANCH) * d and 2 * SGU_DIM == d

    def row(p):
        return p[:, None, :]

    win = w_in.astype(BF16)
    wab = (win[:, :, :nab].reshape(depth, d, 2, groups, LANES)
           .transpose(0, 1, 3, 2, 4).reshape(depth, d, nab))
    gb = gate_b[:, :, None, :]
    gqk = row(jnp.concatenate([jnp.tile(q_norm_g * (LOG2E * HEAD_DIM ** -0.5), (1, N_HEADS)),
                               jnp.tile(k_norm_g, (1, N_KV_HEADS))], axis=1))
    sink = attn_sink * LOG2E
    sw = sgu_w.reshape(depth, SGU_GROUPS // 2, 2, SGU_CHUNK, SGU_CHUNK)
    sw = jnp.concatenate([sw[:, :, 0], sw[:, :, 1]], axis=-1).astype(BF16)
    sb = jnp.repeat(jnp.swapaxes(sgu_b, 1, 2), SGU_DIM // SGU_GROUPS, axis=2)
    g1 = row(norm1_g)
    g2 = row(norm2_g)
    merge_w = (wab, win, gb, conv_dw_w, row(conv_dw_b), row(conv_ln_g), row(conv_ln_b),
               conv_out.astype(BF16), attn_out.astype(BF16), row(sgu_ln_g), row(sgu_ln_b), sw, sb,
               sgu_out.astype(BF16), w_o.astype(BF16))
    ffn_w = (ffn_up.astype(BF16), ffn_dw_w, row(ffn_dw_b), ffn_down.astype(BF16))

    for l in range(depth):
        last = l == depth - 1
        mod_x = mods[l, :bsz]
        mod_c = jnp.broadcast_to(mods[l, bsz][None], (bsz, 6, d))
        qc, kc, vc = _in_proj(ctx, mod_c, g1, win, e, gqk, no_rope, no_rope, rope=False, layer=l)
        qx, kx, vx = _in_proj(x, mod_x, g1, win, e, gqk, cos, sin, rope=True, layer=l)
        ax = _attention(sink, qx, kc, vc, kx, vx, layer=l)
        x_mid = _merge(x, mod_x, g1, ax, *merge_w, layer=l)
        x_new = _conv_ffn(x_mid, mod_x, g2, *ffn_w, layer=l)
        if not last:
            ac = _attention(sink, qc, kc, vc, layer=l)
            c_mid = _merge(ctx, mod_c, g1, ac, *merge_w, layer=l)
            ctx = _conv_ffn(c_mid, mod_c, g2, *ffn_w, layer=l)
        x = x_new
    return x
```

```python
import functools

import jax
import jax.numpy as jnp
import numpy as np
from jax import lax
from jax.experimental import pallas as pl
from jax.experimental.pallas import tpu as pltpu

F32 = jnp.float32
BF16 = jnp.bfloat16

EPS = 1e-6
NEG_INF = -1e30
ROPE_BASE = 10000.0
LOG2E = 1.4426950408889634
GRID_W = 64
HEAD_DIM = 64
N_HEADS = 8
N_KV_HEADS = 2
WINDOW = 128
CONV_DIM = 512
CONV_WIDTH = 31
SGU_DIM = 512
SGU_GROUPS = 8
SGU_CHUNK = 128
FFN_CONV_WIDTH = 3
N_BRANCH = 3

LANES = 128
SUBLANES = 8
CONV_HALO = 16
FFN_HALO = 8
FFN_CHUNK = 256
FFN_ROWS = 256
CONV_ROWS = 64
VMEM_LIMIT = 56 * 1024 * 1024

TM_IN = 1024
TQ_ATTN = 256
TM_MERGE = 512
TM_FFN = 1024


def _cparams(n_axes):
    return pltpu.CompilerParams(
        dimension_semantics=("parallel",) * n_axes, vmem_limit_bytes=VMEM_LIMIT)


def _dot(a, b):
    return jnp.dot(a, b, preferred_element_type=F32)


def _dot_nt(a, b):
    return lax.dot_general(a, b, (((1,), (1,)), ((), ())), preferred_element_type=F32)


def _norm_mod(x, g, shift, scale):
    ms = jnp.mean(x * x, axis=-1, keepdims=True)
    y = x * lax.rsqrt(ms + EPS) * g
    return y * (1.0 + scale) + shift


def _layer_norm(x, g, b):
    mu = jnp.mean(x, axis=-1, keepdims=True)
    xc = x - mu
    var = jnp.mean(xc * xc, axis=-1, keepdims=True)
    return xc * lax.rsqrt(var + EPS) * g + b


def _low_half_lanes():
    return lax.broadcasted_iota(jnp.int32, (1, LANES), 1) < (LANES // 2)


def _full_spec(a):
    return pl.BlockSpec(a.shape, lambda bi, i: (0,) * a.ndim)


def _mod_spec(mods, layer, row):
    blk = (None, 1) + mods.shape[2:]
    if row is None:
        return pl.BlockSpec(blk, lambda bi, i: (layer, bi, 0, 0))
    return pl.BlockSpec(blk, lambda bi, i: (layer, row, 0, 0))


def _layer_spec(a, layer):
    return pl.BlockSpec((None,) + a.shape[1:], lambda bi, i: (layer,) + (0,) * (a.ndim - 1),
                        pipeline_mode=pl.Buffered(1))


def _mod_kernel(c_ref, w_ref, b_ref, o_ref):
    c = c_ref[...]
    s = c * jax.nn.sigmoid(c)
    w = w_ref[0]
    s_hi = s.astype(BF16)
    s_lo = (s - s_hi.astype(F32)).astype(BF16)
    w_hi = w.astype(BF16)
    w_lo = (w - w_hi.astype(F32)).astype(BF16)
    o_ref[0] = _dot(s_hi, w_hi) + _dot(s_lo, w_hi) + _dot(s_hi, w_lo) + b_ref[0]


def _modulation(cond, ada_w, ada_b):
    depth, d, n6 = ada_w.shape
    r = cond.shape[0]
    tn = 1536
    return pl.pallas_call(
        _mod_kernel,
        grid=(depth, n6 // tn),
        in_specs=[
            pl.BlockSpec((r, d), lambda l, j: (0, 0)),
            pl.BlockSpec((1, d, tn), lambda l, j: (l, 0, j)),
            pl.BlockSpec((1, 1, tn), lambda l, j: (l, 0, j)),
        ],
        out_specs=pl.BlockSpec((1, r, tn), lambda l, j: (l, 0, j)),
        out_shape=jax.ShapeDtypeStruct((depth, r, n6), F32),
        compiler_params=_cparams(2),
        name="modulation",
    )(cond, ada_w, ada_b.reshape(depth, 1, n6))


def _in_kernel(x_ref, mod_ref, g_ref, w_ref, e_ref, gqk_ref, cos_ref, sin_ref,
               q_ref, k_ref, v_ref, *, rope):
    x = x_ref[0]
    h = _norm_mod(x, g_ref[...], mod_ref[0, 0:1, :], mod_ref[0, 1:2, :]).astype(BF16)
    nab = 2 * CONV_DIM
    qkv = _dot(h, w_ref[:, nab:nab + (N_HEADS + 2 * N_KV_HEADS) * HEAD_DIM])
    nq = N_HEADS * HEAD_DIM
    nqk = nq + N_KV_HEADS * HEAD_DIM
    e = e_ref[...]
    if rope:
        lane = lax.broadcasted_iota(jnp.int32, (1, LANES), 1)
        first_half = (lane % (HEAD_DIM // 2)) < (HEAD_DIM // 4)
        cos = cos_ref[...]
        sin = sin_ref[...]
    outs = []
    for j in range(nqk // LANES):
        t = qkv[:, j * LANES:(j + 1) * LANES]
        sq = t * t
        hi = sq.astype(BF16)
        lo = (sq - hi.astype(F32)).astype(BF16)
        ss = _dot(jnp.concatenate([hi, lo], axis=1), e)
        tn = t * lax.rsqrt(ss * (1.0 / HEAD_DIM) + EPS) * gqk_ref[:, j * LANES:(j + 1) * LANES]
        if rope:
            quarter = HEAD_DIM // 4
            partner = jnp.where(first_half,
                                pltpu.roll(tn, LANES - quarter, 1),
                                pltpu.roll(tn, quarter, 1))
            tn = tn * cos + partner * sin
        outs.append(tn)
    q_ref[0] = jnp.concatenate(outs[:nq // LANES], axis=1).astype(BF16)
    low = _low_half_lanes()
    kk = outs[nq // LANES]
    kr = pltpu.roll(kk, LANES // 2, 1)
    k_ref[0] = jnp.concatenate([jnp.where(low, kk, kr), jnp.where(low, kr, kk)], axis=1).astype(BF16)
    vv = qkv[:, nqk:]
    vr = pltpu.roll(vv, LANES // 2, 1)
    v_ref[0] = jnp.concatenate([jnp.where(low, vv, 0.0), jnp.where(low, 0.0, vr),
                                jnp.where(low, vr, 0.0), jnp.where(low, 0.0, vv)], axis=1).astype(BF16)


def _in_proj(x, mod, g, w, e, gqk, cos, sin, *, rope, layer, mod_row=None):
    b, l, d = x.shape
    tm = min(TM_IN, l)
    nq = N_HEADS * HEAD_DIM
    nk = 2 * N_KV_HEADS * HEAD_DIM
    nv = 4 * N_KV_HEADS * HEAD_DIM
    return pl.pallas_call(
        functools.partial(_in_kernel, rope=rope),
        grid=(b, l // tm),
        in_specs=[
            pl.BlockSpec((1, tm, d), lambda bi, i: (bi, i, 0)),
            _mod_spec(mod, layer, mod_row),
            _layer_spec(g, layer), _layer_spec(w, layer), _full_spec(e), _layer_spec(gqk, layer),
            pl.BlockSpec((tm, LANES), lambda bi, i: (i, 0)),
            pl.BlockSpec((tm, LANES), lambda bi, i: (i, 0)),
        ],
        out_specs=[
            pl.BlockSpec((1, tm, nq), lambda bi, i: (bi, i, 0)),
            pl.BlockSpec((1, tm, nk), lambda bi, i: (bi, i, 0)),
            pl.BlockSpec((1, tm, nv), lambda bi, i: (bi, i, 0)),
        ],
        out_shape=[
            jax.ShapeDtypeStruct((b, l, nq), BF16),
            jax.ShapeDtypeStruct((b, l, nk), BF16),
            jax.ShapeDtypeStruct((b, l, nv), BF16),
        ],
        compiler_params=_cparams(2),
        name="in_proj_rope" if rope else "in_proj_ctx",
    )(x, mod, g, w, e, gqk, cos, sin)


def _attn_kernel(sink_ref, q_ref, kc_ref, vc_ref, *rest, local, tq, seq, layer):
    if local:
        k_ref, v_ref, o_ref, bias_ref = rest
        i = pl.program_id(1)
        width = tq + 2 * WINDOW
        t0 = i * tq
        start = pl.multiple_of(jnp.clip(t0 - WINDOW, 0, seq - width), WINDOW)
        qpos = t0 + lax.broadcasted_iota(jnp.int32, (tq, 1), 0)
        kpos = start + lax.broadcasted_iota(jnp.int32, (1, width), 1)
        bias_ref[...] = jnp.where(jnp.abs(qpos - kpos) <= WINDOW, 0.0, NEG_INF)
    else:
        (o_ref,) = rest
    low = _low_half_lanes()
    grp = N_HEADS // N_KV_HEADS
    n_ctx = kc_ref.shape[1]

    def scores(h):
        hk = h // grp
        qp = q_ref[0, :, (h // 2) * LANES:(h // 2 + 1) * LANES]
        qh = jnp.where(low, qp, 0.0) if h % 2 == 0 else jnp.where(low, 0.0, qp)
        s = _dot_nt(qh, kc_ref[0, :, hk * LANES:(hk + 1) * LANES])
        if local:
            s_loc = _dot_nt(qh, k_ref[0, pl.ds(start, width), hk * LANES:(hk + 1) * LANES])
            s = jnp.concatenate([s, s_loc + bias_ref[...]], axis=1)
        return s

    def softmax(h, s):
        sink = sink_ref[layer, h]
        m = jnp.maximum(jnp.max(s, axis=-1, keepdims=True), sink)
        p = jnp.exp2(s - m)
        den = jnp.sum(p, axis=-1, keepdims=True) + jnp.exp2(sink - m)
        return p.astype(BF16), 1.0 / den

    def values(h, p, inv):
        vs = slice((2 * (h // grp) + h % 2) * LANES, (2 * (h // grp) + h % 2 + 1) * LANES)
        o = _dot(p[:, :n_ctx], vc_ref[0, :, vs])
        if local:
            o = o + _dot(p[:, n_ctx:], v_ref[0, pl.ds(start, width), vs])
        return o * inv

    s = scores(0)
    o_pair = None
    for h in range(N_HEADS):
        s_next = scores(h + 1) if h + 1 < N_HEADS else None
        p, inv = softmax(h, s)
        o = values(h, p, inv)
        if h % 2 == 0:
            o_pair = o
        else:
            o_ref[0, :, (h // 2) * LANES:(h // 2 + 1) * LANES] = (o_pair + o).astype(o_ref.dtype)
        s = s_next


def _attention(sink, q, kc, vc, k=None, v=None, *, layer):
    b, l, nq = q.shape
    lc = kc.shape[1]
    local = k is not None
    tq = min(TQ_ATTN, l)
    if local:
        assert l >= tq + 2 * WINDOW and tq % WINDOW == 0
    in_specs = [
        pl.BlockSpec(memory_space=pltpu.SMEM),
        pl.BlockSpec((1, tq, nq), lambda bi, i: (bi, i, 0)),
        pl.BlockSpec((1, lc, kc.shape[2]), lambda bi, i: (bi, 0, 0)),
        pl.BlockSpec((1, lc, vc.shape[2]), lambda bi, i: (bi, 0, 0)),
    ]
    args = [sink, q, kc, vc]
    if local:
        in_specs += [pl.BlockSpec((1, l, k.shape[2]), lambda bi, i: (bi, 0, 0)),
                     pl.BlockSpec((1, l, v.shape[2]), lambda bi, i: (bi, 0, 0))]
        args += [k, v]
    return pl.pallas_call(
        functools.partial(_attn_kernel, local=local, tq=tq, seq=l, layer=layer),
        grid=(b, l // tq),
        in_specs=in_specs,
        out_specs=pl.BlockSpec((1, tq, nq), lambda bi, i: (bi, i, 0)),
        out_shape=jax.ShapeDtypeStruct((b, l, nq), BF16),
        scratch_shapes=[pltpu.VMEM((tq, tq + 2 * WINDOW), F32)] if local else [],
        compiler_params=_cparams(2),
        name="attn_window" if local else "attn_ctx",
    )(*args)


def _merge_kernel(x_ref, xp_ref, xn_ref, mod_ref, g_ref, a_ref,
                  wab_ref, win_ref, gb_ref,
                  cw_ref, cb_ref, clg_ref, clb_ref, cout_ref, aout_ref,
                  slg_ref, slb_ref, sw_ref, sb_ref, sout_ref, wo_ref,
                  o_ref, h_ref, ys_ref, yc_ref, *, tm, rows):
    i = pl.program_id(1)
    n = pl.num_programs(1)
    g = g_ref[...]
    shift = mod_ref[0, 0:1, :]
    scale = mod_ref[0, 1:2, :]
    x = x_ref[0]
    y_attn = _dot(a_ref[0], aout_ref[...])
    h_ref[...] = jnp.concatenate(
        [jnp.where(i > 0, _norm_mod(xp_ref[0], g, shift, scale), 0.0),
         _norm_mod(x, g, shift, scale),
         jnp.where(i < n - 1, _norm_mod(xn_ref[0], g, shift, scale), 0.0)], axis=0).astype(BF16)
    ext = tm + 2 * CONV_HALO
    base = CONV_HALO - CONV_WIDTH // 2
    span = (base + CONV_WIDTH - 1) // SUBLANES * SUBLANES
    n_groups = CONV_DIM // LANES

    def glu_proj(c):
        ab = _dot(h_ref[...], wab_ref[:, 2 * c * LANES:2 * (c + 1) * LANES])
        y = ab[:, :LANES] * jax.nn.sigmoid(ab[:, LANES:])
        ys_ref[c % 2, 0] = y
        for r in range(1, SUBLANES):
            ys_ref[c % 2, r] = pltpu.roll(y, ext - r, 0)

    def taps(c):
        cols = slice(c * LANES, (c + 1) * LANES)
        for r0 in range(0, tm, rows):
            acc = jnp.broadcast_to(cb_ref[:, cols], (rows, LANES))
            for r in range(SUBLANES):
                slab = ys_ref[c % 2, r, r0:r0 + rows + span, :]
                for t in range(CONV_WIDTH):
                    if (base + t) % SUBLANES == r:
                        lo = base + t - r
                        acc = acc + cw_ref[t:t + 1, cols] * slab[lo:lo + rows]
            yc_ref[r0:r0 + rows, cols] = acc

    def h_main():
        return h_ref[CONV_HALO:CONV_HALO + tm, :]

    d = x.shape[-1]
    rest = 2 * CONV_DIM + (N_HEADS + 2 * N_KV_HEADS) * HEAD_DIM

    def h_proj(k):
        out = _dot(h_main(), win_ref[:, rest + k * d:rest + (k + 1) * d])
        return out if k == 0 else out + gb_ref[k - 1]

    n_pre = 1 + N_BRANCH
    pre = []
    glu_proj(0)
    for c in range(n_groups):
        if c + 1 < n_groups:
            glu_proj(c + 1)
        if len(pre) < n_pre:
            pre.append(h_proj(len(pre)))
        taps(c)
    while len(pre) < n_pre:
        pre.append(h_proj(len(pre)))
    uv, gates = pre[0], pre[1:]

    u = jax.nn.gelu(uv[:, :SGU_DIM])
    v = _layer_norm(jax.nn.gelu(uv[:, SGU_DIM:]), slg_ref[...], slb_ref[...])
    low = _low_half_lanes()
    mixed = []
    for c in range(tm // SGU_CHUNK):
        cols = []
        for j in range(SGU_DIM // LANES):
            vj = v[c * SGU_CHUNK:(c + 1) * SGU_CHUNK, j * LANES:(j + 1) * LANES]
            stacked = jnp.concatenate(
                [jnp.where(low, vj, 0.0), jnp.where(low, 0.0, vj)], axis=0).astype(BF16)
            cols.append(_dot(sw_ref[j], stacked))
        mixed.append(jnp.concatenate(cols, axis=1) + sb_ref[...])
    mixed = jnp.concatenate(mixed, axis=0) if len(mixed) > 1 else mixed[0]
    y_sgu = _dot((u * mixed).astype(BF16), sout_ref[...])

    yc = _layer_norm(yc_ref[...], clg_ref[...], clb_ref[...])
    yc = yc * jax.nn.sigmoid(yc)
    y_conv = _dot(yc.astype(BF16), cout_ref[...])
    m = (jax.nn.sigmoid(gates[0]) * y_conv + jax.nn.sigmoid(gates[1]) * y_attn
         + jax.nn.sigmoid(gates[2]) * y_sgu)

    o_ref[0] = x + mod_ref[0, 2:3, :] * _dot(m.astype(BF16), wo_ref[...])


def _merge(x, mod, g, attn, wab, win, gb, cw, cb, clg, clb, cout, aout,
           slg, slb, sw, sb, sout, wo, *, layer, mod_row=None):
    b, l, d = x.shape
    tm = min(TM_MERGE, l)
    hb = tm // CONV_HALO
    nhb = l // CONV_HALO
    weights = (wab, win, gb, cw, cb, clg, clb, cout, aout, slg, slb, sw, sb, sout, wo)
    return pl.pallas_call(
        functools.partial(_merge_kernel, tm=tm, rows=CONV_ROWS),
        grid=(b, l // tm),
        in_specs=[
            pl.BlockSpec((1, tm, d), lambda bi, i: (bi, i, 0)),
            pl.BlockSpec((1, CONV_HALO, d), lambda bi, i: (bi, jnp.maximum(i * hb - 1, 0), 0)),
            pl.BlockSpec((1, CONV_HALO, d),
                         lambda bi, i: (bi, jnp.minimum((i + 1) * hb, nhb - 1), 0)),
            _mod_spec(mod, layer, mod_row),
            _layer_spec(g, layer),
            pl.BlockSpec((1, tm, attn.shape[-1]), lambda bi, i: (bi, i, 0)),
        ] + [_layer_spec(a, layer) for a in weights],
        out_specs=pl.BlockSpec((1, tm, d), lambda bi, i: (bi, i, 0)),
        out_shape=jax.ShapeDtypeStruct((b, l, d), F32),
        scratch_shapes=[
            pltpu.VMEM((tm + 2 * CONV_HALO, d), BF16),
            pltpu.VMEM((2, SUBLANES, tm + 2 * CONV_HALO, LANES), F32),
            pltpu.VMEM((tm, CONV_DIM), F32),
        ],
        compiler_params=_cparams(2),
        name="mixer_merge",
    )(x, x, x, mod, g, attn, *weights)


def _ffn_kernel(x_ref, xp_ref, xn_ref, mod_ref, g_ref, up_ref, dw_ref, db_ref, down_ref,
                o_ref, h_ref, z_ref, act_ref, *, tm, ffn, rows):
    i = pl.program_id(1)
    n = pl.num_programs(1)
    g = g_ref[...]
    shift = mod_ref[0, 3:4, :]
    scale = mod_ref[0, 4:5, :]
    x = x_ref[0]
    h_ref[...] = jnp.concatenate(
        [jnp.where(i > 0, _norm_mod(xp_ref[0], g, shift, scale), 0.0),
         _norm_mod(x, g, shift, scale),
         jnp.where(i < n - 1, _norm_mod(xn_ref[0], g, shift, scale), 0.0)], axis=0).astype(BF16)
    ck = FFN_CHUNK
    n_chunks = ffn // ck
    ext = rows + 2 * FFN_HALO

    def offsets(c):
        return (c * ck, ffn + c * ck)

    def up_proj(c):
        for half, off in enumerate(offsets(c)):
            z_ref[c % 2, half] = _dot(h_ref[...], up_ref[:, off:off + ck])

    def conv_gate(c):
        for r0 in range(0, tm, rows):
            halves = []
            for half, off in enumerate(offsets(c)):
                z = z_ref[c % 2, half, r0:r0 + ext, :]
                halves.append(
                    db_ref[:, off:off + ck]
                    + dw_ref[0:1, off:off + ck] * pltpu.roll(z, 1, 0)[FFN_HALO:FFN_HALO + rows]
                    + dw_ref[1:2, off:off + ck] * z[FFN_HALO:FFN_HALO + rows]
                    + dw_ref[2:3, off:off + ck] * pltpu.roll(z, ext - 1, 0)[FFN_HALO:FFN_HALO + rows])
            a, gate = halves
            act_ref[r0:r0 + rows, c * ck:(c + 1) * ck] = (a * jax.nn.sigmoid(a) * gate).astype(BF16)

    up_proj(0)
    for c in range(n_chunks):
        if c + 1 < n_chunks:
            up_proj(c + 1)
        conv_gate(c)
    o_ref[0] = x + mod_ref[0, 5:6, :] * _dot(act_ref[...], down_ref[...])


def _conv_ffn(x, mod, g, up, dw, db, down, *, layer, mod_row=None):
    b, l, d = x.shape
    tm = min(TM_FFN, l)
    ffn = down.shape[1]
    hb = tm // FFN_HALO
    nhb = l // FFN_HALO
    return pl.pallas_call(
        functools.partial(_ffn_kernel, tm=tm, ffn=ffn, rows=min(FFN_ROWS, tm)),
        grid=(b, l // tm),
        in_specs=[
            pl.BlockSpec((1, tm, d), lambda bi, i: (bi, i, 0)),
            pl.BlockSpec((1, FFN_HALO, d), lambda bi, i: (bi, jnp.maximum(i * hb - 1, 0), 0)),
            pl.BlockSpec((1, FFN_HALO, d),
                         lambda bi, i: (bi, jnp.minimum((i + 1) * hb, nhb - 1), 0)),
            _mod_spec(mod, layer, mod_row),
            _layer_spec(g, layer), _layer_spec(up, layer), _layer_spec(dw, layer), _layer_spec(db, layer),
            _layer_spec(down, layer),
        ],
        out_specs=pl.BlockSpec((1, tm, d), lambda bi, i: (bi, i, 0)),
        out_shape=jax.ShapeDtypeStruct((b, l, d), F32),
        scratch_shapes=[
            pltpu.VMEM((tm + 2 * FFN_HALO, d), BF16),
            pltpu.VMEM((2, 2, tm + 2 * FFN_HALO, FFN_CHUNK), F32),
            pltpu.VMEM((tm, ffn), BF16),
        ],
        compiler_params=_cparams(2),
        name="conv_ffn",
    )(x, x, x, mod, g, up, dw, db, down)


def _rope_tables(seq):
    quarter = HEAD_DIM // 4
    inv = np.power(ROPE_BASE, -np.arange(quarter, dtype=np.float64) / quarter)
    t = np.arange(seq)
    rows = (t // GRID_W)[:, None] * inv[None, :]
    cols = (t % GRID_W)[:, None] * inv[None, :]
    cos = np.concatenate([np.cos(rows)] * 2 + [np.cos(cols)] * 2, axis=1)
    sin = np.concatenate([-np.sin(rows), np.sin(rows), -np.sin(cols), np.sin(cols)], axis=1)
    reps = LANES // HEAD_DIM
    return (np.tile(cos, (1, reps)).astype(np.float32), np.tile(sin, (1, reps)).astype(np.float32))


def kernel(x, c, ctx, c_ctx, ada_w, ada_b, norm1_g, norm2_g, w_in, gate_b, conv_dw_w, conv_dw_b, conv_ln_g, conv_ln_b, conv_out, q_norm_g, k_norm_g, attn_sink, attn_out, sgu_ln_g, sgu_ln_b, sgu_w, sgu_b, sgu_out, w_o, ffn_up, ffn_dw_w, ffn_dw_b, ffn_down):
    bsz, seq, d = x.shape
    depth = ada_w.shape[0]
    n_ctx = ctx.shape[1]

    n_rows = -(-(bsz + 1) // SUBLANES) * SUBLANES
    cond = jnp.concatenate([c, c_ctx[None], jnp.zeros((n_rows - bsz - 1, d), F32)], axis=0)
    mods = _modulation(cond, ada_w, ada_b).reshape(depth, n_rows, 6, d)

    cos, sin = _rope_tables(seq)
    no_rope = np.zeros((n_ctx, LANES), np.float32)
    head_id = np.arange(LANES) // HEAD_DIM
    e = (head_id[:, None] == head_id[None, :]).astype(np.float32)
    e = jnp.asarray(np.concatenate([e, e], axis=0), BF16)

    nab = 2 * CONV_DIM
    n_in1 = nab + (N_HEADS + 2 * N_KV_HEADS) * HEAD_DIM
    groups = CONV_DIM // LANES
    assert w_in.shape[2] - n_in1 == (1 + N_BRANCH) * d and 2 * SGU_DIM == d

    def row(p):
        return p[:, None, :]

    win = w_in.astype(BF16)
    wab = (win[:, :, :nab].reshape(depth, d, 2, groups, LANES)
           .transpose(0, 1, 3, 2, 4).reshape(depth, d, nab))
    gb = gate_b[:, :, None, :]
    gqk = row(jnp.concatenate([jnp.tile(q_norm_g * (LOG2E * HEAD_DIM ** -0.5), (1, N_HEADS)),
                               jnp.tile(k_norm_g, (1, N_KV_HEADS))], axis=1))
    sink = attn_sink * LOG2E
    sw = sgu_w.reshape(depth, SGU_GROUPS // 2, 2, SGU_CHUNK, SGU_CHUNK)
    sw = jnp.concatenate([sw[:, :, 0], sw[:, :, 1]], axis=-1).astype(BF16)
    sb = jnp.repeat(jnp.swapaxes(sgu_b, 1, 2), SGU_DIM // SGU_GROUPS, axis=2)
    g1 = row(norm1_g)
    g2 = row(norm2_g)
    merge_w = (wab, win, gb, conv_dw_w, row(conv_dw_b), row(conv_ln_g), row(conv_ln_b),
               conv_out.astype(BF16), attn_out.astype(BF16), row(sgu_ln_g), row(sgu_ln_b), sw, sb,
               sgu_out.astype(BF16), w_o.astype(BF16))
    ffn_w = (ffn_up.astype(BF16), ffn_dw_w, row(ffn_dw_b), ffn_down.astype(BF16))

    for l in range(depth):
        last = l == depth - 1
        qc, kc, vc = _in_proj(ctx, mods, g1, win, e, gqk, no_rope, no_rope, rope=False, layer=l,
                              mod_row=bsz)
        qx, kx, vx = _in_proj(x, mods, g1, win, e, gqk, cos, sin, rope=True, layer=l)
        ax = _attention(sink, qx, kc, vc, kx, vx, layer=l)
        x_mid = _merge(x, mods, g1, ax, *merge_w, layer=l)
        x_new = _conv_ffn(x_mid, mods, g2, *ffn_w, layer=l)
        if not last:
            ac = _attention(sink, qc, kc, vc, layer=l)
            c_mid = _merge(ctx, mods, g1, ac, *merge_w, layer=l, mod_row=bsz)
            ctx = _conv_ffn(c_mid, mods, g2, *ffn_w, layer=l, mod_row=bsz)
        x = x_new
    return x
```

```python
import functools

import jax
import jax.numpy as jnp
import numpy as np
from jax import lax
from jax.experimental import pallas as pl
from jax.experimental.pallas import tpu as pltpu

F32 = jnp.float32
BF16 = jnp.bfloat16

EPS = 1e-6
NEG_INF = -1e30
ROPE_BASE = 10000.0
LOG2E = 1.4426950408889634
GRID_W = 64
HEAD_DIM = 64
N_HEADS = 8
N_KV_HEADS = 2
WINDOW = 128
CONV_DIM = 512
CONV_WIDTH = 31
SGU_DIM = 512
SGU_GROUPS = 8
SGU_CHUNK = 128
FFN_CONV_WIDTH = 3
N_BRANCH = 3

LANES = 128
SUBLANES = 8
CONV_HALO = 16
FFN_HALO = 8
FFN_CHUNK = 256
FFN_ROWS = 256
CONV_ROWS = 64
VMEM_LIMIT = 56 * 1024 * 1024

TM_IN = 1024
TQ_ATTN = 256
TM_MERGE = 512
TM_FFN = 1024


def _cparams(n_axes):
    return pltpu.CompilerParams(
        dimension_semantics=("parallel",) * n_axes, vmem_limit_bytes=VMEM_LIMIT)


def _dot(a, b):
    return jnp.dot(a, b, preferred_element_type=F32)


def _dot_nt(a, b):
    return lax.dot_general(a, b, (((1,), (1,)), ((), ())), preferred_element_type=F32)


def _norm_mod(x, g, shift, scale):
    ms = jnp.mean(x * x, axis=-1, keepdims=True)
    return x * lax.rsqrt(ms + EPS) * (g * (1.0 + scale)) + shift


def _layer_norm(x, g, b):
    mu = jnp.mean(x, axis=-1, keepdims=True)
    xc = x - mu
    var = jnp.mean(xc * xc, axis=-1, keepdims=True)
    return xc * lax.rsqrt(var + EPS) * g + b


def _low_half_lanes():
    return lax.broadcasted_iota(jnp.int32, (1, LANES), 1) < (LANES // 2)


def _full_spec(a):
    return pl.BlockSpec(a.shape, lambda bi, i: (0,) * a.ndim)


def _mod_spec(mods, layer, row):
    blk = (None, 1) + mods.shape[2:]
    if row is None:
        return pl.BlockSpec(blk, lambda bi, i: (layer, bi, 0, 0))
    return pl.BlockSpec(blk, lambda bi, i: (layer, row, 0, 0))


def _layer_spec(a, layer):
    return pl.BlockSpec((None,) + a.shape[1:], lambda bi, i: (layer,) + (0,) * (a.ndim - 1),
                        pipeline_mode=pl.Buffered(1))


def _mod_kernel(c_ref, w_ref, b_ref, o_ref):
    c = c_ref[...]
    s = c * jax.nn.sigmoid(c)
    w = w_ref[0]
    s_hi = s.astype(BF16)
    s_lo = (s - s_hi.astype(F32)).astype(BF16)
    w_hi = w.astype(BF16)
    w_lo = (w - w_hi.astype(F32)).astype(BF16)
    o_ref[0] = _dot(s_hi, w_hi) + _dot(s_lo, w_hi) + _dot(s_hi, w_lo) + b_ref[0]


def _modulation(cond, ada_w, ada_b):
    depth, d, n6 = ada_w.shape
    r = cond.shape[0]
    tn = 1536
    return pl.pallas_call(
        _mod_kernel,
        grid=(depth, n6 // tn),
        in_specs=[
            pl.BlockSpec((r, d), lambda l, j: (0, 0)),
            pl.BlockSpec((1, d, tn), lambda l, j: (l, 0, j)),
            pl.BlockSpec((1, 1, tn), lambda l, j: (l, 0, j)),
        ],
        out_specs=pl.BlockSpec((1, r, tn), lambda l, j: (l, 0, j)),
        out_shape=jax.ShapeDtypeStruct((depth, r, n6), F32),
        compiler_params=_cparams(2),
        name="modulation",
    )(cond, ada_w, ada_b.reshape(depth, 1, n6))


def _in_kernel(x_ref, mod_ref, g_ref, w_ref, e_ref, gqk_ref, cos_ref, sin_ref,
               q_ref, k_ref, v_ref, *, rope):
    x = x_ref[0]
    h = _norm_mod(x, g_ref[...], mod_ref[0, 0:1, :], mod_ref[0, 1:2, :]).astype(BF16)
    qkv = _dot(h, w_ref[:, :(N_HEADS + 2 * N_KV_HEADS) * HEAD_DIM])
    nq = N_HEADS * HEAD_DIM
    nqk = nq + N_KV_HEADS * HEAD_DIM
    e = e_ref[...]
    if rope:
        lane = lax.broadcasted_iota(jnp.int32, (1, LANES), 1)
        first_half = (lane % (HEAD_DIM // 2)) < (HEAD_DIM // 4)
        cos = cos_ref[...]
        sin = sin_ref[...]
    outs = []
    for j in range(nqk // LANES):
        t = qkv[:, j * LANES:(j + 1) * LANES]
        sq = t * t
        hi = sq.astype(BF16)
        lo = (sq - hi.astype(F32)).astype(BF16)
        ss = _dot(jnp.concatenate([hi, lo], axis=1), e)
        tn = t * lax.rsqrt(ss * (1.0 / HEAD_DIM) + EPS) * gqk_ref[:, j * LANES:(j + 1) * LANES]
        if rope:
            quarter = HEAD_DIM // 4
            partner = jnp.where(first_half,
                                pltpu.roll(tn, LANES - quarter, 1),
                                pltpu.roll(tn, quarter, 1))
            tn = tn * cos + partner * sin
        outs.append(tn)
    q_ref[0] = jnp.concatenate(outs[:nq // LANES], axis=1).astype(BF16)
    low = _low_half_lanes()
    kk = outs[nq // LANES]
    kr = pltpu.roll(kk, LANES // 2, 1)
    k_ref[0] = jnp.concatenate([jnp.where(low, kk, kr), jnp.where(low, kr, kk)], axis=1).astype(BF16)
    vv = qkv[:, nqk:]
    vr = pltpu.roll(vv, LANES // 2, 1)
    v_ref[0] = jnp.concatenate([jnp.where(low, vv, 0.0), jnp.where(low, 0.0, vr),
                                jnp.where(low, vr, 0.0), jnp.where(low, 0.0, vv)], axis=1).astype(BF16)


def _in_proj(x, mod, g, w, e, gqk, cos, sin, *, rope, layer, mod_row=None):
    b, l, d = x.shape
    tm = min(TM_IN, l)
    nq = N_HEADS * HEAD_DIM
    nk = 2 * N_KV_HEADS * HEAD_DIM
    nv = 4 * N_KV_HEADS * HEAD_DIM
    assert (N_HEADS + 2 * N_KV_HEADS) * HEAD_DIM <= 2 * CONV_DIM
    return pl.pallas_call(
        functools.partial(_in_kernel, rope=rope),
        grid=(b, l // tm),
        in_specs=[
            pl.BlockSpec((1, tm, d), lambda bi, i: (bi, i, 0)),
            _mod_spec(mod, layer, mod_row),
            _layer_spec(g, layer),
            pl.BlockSpec((None, d, 2 * CONV_DIM), lambda bi, i: (layer, 0, 1),
                         pipeline_mode=pl.Buffered(1)),
            _full_spec(e), _layer_spec(gqk, layer),
            pl.BlockSpec((tm, LANES), lambda bi, i: (i, 0)),
            pl.BlockSpec((tm, LANES), lambda bi, i: (i, 0)),
        ],
        out_specs=[
            pl.BlockSpec((1, tm, nq), lambda bi, i: (bi, i, 0)),
            pl.BlockSpec((1, tm, nk), lambda bi, i: (bi, i, 0)),
            pl.BlockSpec((1, tm, nv), lambda bi, i: (bi, i, 0)),
        ],
        out_shape=[
            jax.ShapeDtypeStruct((b, l, nq), BF16),
            jax.ShapeDtypeStruct((b, l, nk), BF16),
            jax.ShapeDtypeStruct((b, l, nv), BF16),
        ],
        compiler_params=_cparams(2),
        name="in_proj_rope" if rope else "in_proj_ctx",
    )(x, mod, g, w, e, gqk, cos, sin)


def _attn_kernel(sink_ref, q_ref, kc_ref, vc_ref, *rest, local, tq, seq, layer):
    if local:
        k_ref, v_ref, o_ref, bias_ref = rest
        i = pl.program_id(1)
        width = tq + 2 * WINDOW
        t0 = i * tq
        start = pl.multiple_of(jnp.clip(t0 - WINDOW, 0, seq - width), WINDOW)
        qpos = t0 + lax.broadcasted_iota(jnp.int32, (tq, 1), 0)
        kpos = start + lax.broadcasted_iota(jnp.int32, (1, width), 1)
        bias_ref[...] = jnp.where(jnp.abs(qpos - kpos) <= WINDOW, 0.0, NEG_INF)
    else:
        (o_ref,) = rest
    low = _low_half_lanes()
    grp = N_HEADS // N_KV_HEADS
    n_ctx = kc_ref.shape[1]

    def scores(h):
        hk = h // grp
        qp = q_ref[0, :, (h // 2) * LANES:(h // 2 + 1) * LANES]
        qh = jnp.where(low, qp, 0.0) if h % 2 == 0 else jnp.where(low, 0.0, qp)
        s = _dot_nt(qh, kc_ref[0, :, hk * LANES:(hk + 1) * LANES])
        if local:
            s_loc = _dot_nt(qh, k_ref[0, pl.ds(start, width), hk * LANES:(hk + 1) * LANES])
            s = jnp.concatenate([s, s_loc + bias_ref[...]], axis=1)
        return s

    def softmax(h, s):
        sink = sink_ref[layer, h]
        m = jnp.maximum(jnp.max(s, axis=-1, keepdims=True), sink)
        p = jnp.exp2(s - m)
        den = jnp.sum(p, axis=-1, keepdims=True) + jnp.exp2(sink - m)
        return p.astype(BF16), 1.0 / den

    def values(h, p, inv):
        vs = slice((2 * (h // grp) + h % 2) * LANES, (2 * (h // grp) + h % 2 + 1) * LANES)
        o = _dot(p[:, :n_ctx], vc_ref[0, :, vs])
        if local:
            o = o + _dot(p[:, n_ctx:], v_ref[0, pl.ds(start, width), vs])
        return o * inv

    s = scores(0)
    o_pair = None
    for h in range(N_HEADS):
        s_next = scores(h + 1) if h + 1 < N_HEADS else None
        p, inv = softmax(h, s)
        o = values(h, p, inv)
        if h % 2 == 0:
            o_pair = o
        else:
            o_ref[0, :, (h // 2) * LANES:(h // 2 + 1) * LANES] = (o_pair + o).astype(o_ref.dtype)
        s = s_next


def _attention(sink, q, kc, vc, k=None, v=None, *, layer):
    b, l, nq = q.shape
    lc = kc.shape[1]
    local = k is not None
    tq = min(TQ_ATTN, l)
    if local:
        assert l >= tq + 2 * WINDOW and tq % WINDOW == 0
    in_specs = [
        pl.BlockSpec(memory_space=pltpu.SMEM),
        pl.BlockSpec((1, tq, nq), lambda bi, i: (bi, i, 0)),
        pl.BlockSpec((1, lc, kc.shape[2]), lambda bi, i: (bi, 0, 0)),
        pl.BlockSpec((1, lc, vc.shape[2]), lambda bi, i: (bi, 0, 0)),
    ]
    args = [sink, q, kc, vc]
    if local:
        in_specs += [pl.BlockSpec((1, l, k.shape[2]), lambda bi, i: (bi, 0, 0)),
                     pl.BlockSpec((1, l, v.shape[2]), lambda bi, i: (bi, 0, 0))]
        args += [k, v]
    return pl.pallas_call(
        functools.partial(_attn_kernel, local=local, tq=tq, seq=l, layer=layer),
        grid=(b, l // tq),
        in_specs=in_specs,
        out_specs=pl.BlockSpec((1, tq, nq), lambda bi, i: (bi, i, 0)),
        out_shape=jax.ShapeDtypeStruct((b, l, nq), BF16),
        scratch_shapes=[pltpu.VMEM((tq, tq + 2 * WINDOW), F32)] if local else [],
        compiler_params=_cparams(2),
        name="attn_window" if local else "attn_ctx",
    )(*args)


def _merge_kernel(x_ref, xp_ref, xn_ref, mod_ref, g_ref, a_ref,
                  wab_ref, win_ref, gb_ref,
                  cw_ref, cb_ref, clg_ref, clb_ref, cout_ref, aout_ref,
                  slg_ref, slb_ref, sw_ref, sb_ref, sout_ref, wo_ref,
                  o_ref, h_ref, ys_ref, yc_ref, *, tm, rows):
    i = pl.program_id(1)
    n = pl.num_programs(1)
    g = g_ref[...]
    shift = mod_ref[0, 0:1, :]
    scale = mod_ref[0, 1:2, :]
    x = x_ref[0]
    y_attn = _dot(a_ref[0], aout_ref[...])
    h_ref[...] = jnp.concatenate(
        [jnp.where(i > 0, _norm_mod(xp_ref[0], g, shift, scale), 0.0),
         _norm_mod(x, g, shift, scale),
         jnp.where(i < n - 1, _norm_mod(xn_ref[0], g, shift, scale), 0.0)], axis=0).astype(BF16)
    ext = tm + 2 * CONV_HALO
    base = CONV_HALO - CONV_WIDTH // 2
    span = (base + CONV_WIDTH - 1) // SUBLANES * SUBLANES
    n_groups = CONV_DIM // LANES

    def glu_proj(c):
        ab = _dot(h_ref[...], wab_ref[:, 2 * c * LANES:2 * (c + 1) * LANES])
        y = ab[:, :LANES] * jax.nn.sigmoid(ab[:, LANES:])
        ys_ref[c % 2, 0] = y
        for r in range(1, SUBLANES):
            ys_ref[c % 2, r] = pltpu.roll(y, ext - r, 0)

    def taps(c):
        cols = slice(c * LANES, (c + 1) * LANES)
        for r0 in range(0, tm, rows):
            acc = jnp.broadcast_to(cb_ref[:, cols], (rows, LANES))
            for r in range(SUBLANES):
                slab = ys_ref[c % 2, r, r0:r0 + rows + span, :]
                for t in range(CONV_WIDTH):
                    if (base + t) % SUBLANES == r:
                        lo = base + t - r
                        acc = acc + cw_ref[t:t + 1, cols] * slab[lo:lo + rows]
            yc_ref[r0:r0 + rows, cols] = acc

    def h_main():
        return h_ref[CONV_HALO:CONV_HALO + tm, :]

    d = x.shape[-1]
    rest = 2 * CONV_DIM + (N_HEADS + 2 * N_KV_HEADS) * HEAD_DIM

    def h_proj(k):
        out = _dot(h_main(), win_ref[:, rest + k * d:rest + (k + 1) * d])
        return out if k == 0 else out + gb_ref[k - 1]

    n_pre = 1 + N_BRANCH
    pre = []
    glu_proj(0)
    for c in range(n_groups):
        if c + 1 < n_groups:
            glu_proj(c + 1)
        if len(pre) < n_pre:
            pre.append(h_proj(len(pre)))
        taps(c)
    while len(pre) < n_pre:
        pre.append(h_proj(len(pre)))
    uv, gates = pre[0], pre[1:]

    u = jax.nn.gelu(uv[:, :SGU_DIM])
    v = _layer_norm(jax.nn.gelu(uv[:, SGU_DIM:]), slg_ref[...], slb_ref[...])
    low = _low_half_lanes()
    mixed = []
    for c in range(tm // SGU_CHUNK):
        cols = []
        for j in range(SGU_DIM // LANES):
            vj = v[c * SGU_CHUNK:(c + 1) * SGU_CHUNK, j * LANES:(j + 1) * LANES]
            stacked = jnp.concatenate(
                [jnp.where(low, vj, 0.0), jnp.where(low, 0.0, vj)], axis=0).astype(BF16)
            cols.append(_dot(sw_ref[j], stacked))
        mixed.append(jnp.concatenate(cols, axis=1) + sb_ref[...])
    mixed = jnp.concatenate(mixed, axis=0) if len(mixed) > 1 else mixed[0]
    y_sgu = _dot((u * mixed).astype(BF16), sout_ref[...])

    yc = _layer_norm(yc_ref[...], clg_ref[...], clb_ref[...])
    yc = yc * jax.nn.sigmoid(yc)
    y_conv = _dot(yc.astype(BF16), cout_ref[...])
    m = (jax.nn.sigmoid(gates[0]) * y_conv + jax.nn.sigmoid(gates[1]) * y_attn
         + jax.nn.sigmoid(gates[2]) * y_sgu)

    o_ref[0] = x + mod_ref[0, 2:3, :] * _dot(m.astype(BF16), wo_ref[...])


def _merge(x, mod, g, attn, wab, win, gb, cw, cb, clg, clb, cout, aout,
           slg, slb, sw, sb, sout, wo, *, layer, mod_row=None):
    b, l, d = x.shape
    tm = min(TM_MERGE, l)
    hb = tm // CONV_HALO
    nhb = l // CONV_HALO
    weights = (wab, win, gb, cw, cb, clg, clb, cout, aout, slg, slb, sw, sb, sout, wo)
    return pl.pallas_call(
        functools.partial(_merge_kernel, tm=tm, rows=CONV_ROWS),
        grid=(b, l // tm),
        in_specs=[
            pl.BlockSpec((1, tm, d), lambda bi, i: (bi, i, 0)),
            pl.BlockSpec((1, CONV_HALO, d), lambda bi, i: (bi, jnp.maximum(i * hb - 1, 0), 0)),
            pl.BlockSpec((1, CONV_HALO, d),
                         lambda bi, i: (bi, jnp.minimum((i + 1) * hb, nhb - 1), 0)),
            _mod_spec(mod, layer, mod_row),
            _layer_spec(g, layer),
            pl.BlockSpec((1, tm, attn.shape[-1]), lambda bi, i: (bi, i, 0)),
        ] + [_layer_spec(a, layer) for a in weights],
        out_specs=pl.BlockSpec((1, tm, d), lambda bi, i: (bi, i, 0)),
        out_shape=jax.ShapeDtypeStruct((b, l, d), F32),
        scratch_shapes=[
            pltpu.VMEM((tm + 2 * CONV_HALO, d), BF16),
            pltpu.VMEM((2, SUBLANES, tm + 2 * CONV_HALO, LANES), F32),
            pltpu.VMEM((tm, CONV_DIM), F32),
        ],
        compiler_params=_cparams(2),
        name="mixer_merge",
    )(x, x, x, mod, g, attn, *weights)


def _ffn_kernel(x_ref, xp_ref, xn_ref, mod_ref, g_ref, up_ref, dw_ref, db_ref, down_ref,
                o_ref, h_ref, z_ref, act_ref, *, tm, ffn, rows):
    i = pl.program_id(1)
    n = pl.num_programs(1)
    g = g_ref[...]
    shift = mod_ref[0, 3:4, :]
    scale = mod_ref[0, 4:5, :]
    x = x_ref[0]
    h_ref[...] = jnp.concatenate(
        [jnp.where(i > 0, _norm_mod(xp_ref[0], g, shift, scale), 0.0),
         _norm_mod(x, g, shift, scale),
         jnp.where(i < n - 1, _norm_mod(xn_ref[0], g, shift, scale), 0.0)], axis=0).astype(BF16)
    ck = FFN_CHUNK
    n_chunks = ffn // ck
    ext = rows + 2 * FFN_HALO

    def offsets(c):
        return (c * ck, ffn + c * ck)

    def up_proj(c):
        for half, off in enumerate(offsets(c)):
            z_ref[c % 2, half] = _dot(h_ref[...], up_ref[:, off:off + ck])

    def conv_gate(c):
        for r0 in range(0, tm, rows):
            halves = []
            for half, off in enumerate(offsets(c)):
                z = z_ref[c % 2, half, r0:r0 + ext, :]
                halves.append(
                    db_ref[:, off:off + ck]
                    + dw_ref[0:1, off:off + ck] * pltpu.roll(z, 1, 0)[FFN_HALO:FFN_HALO + rows]
                    + dw_ref[1:2, off:off + ck] * z[FFN_HALO:FFN_HALO + rows]
                    + dw_ref[2:3, off:off + ck] * pltpu.roll(z, ext - 1, 0)[FFN_HALO:FFN_HALO + rows])
            a, gate = halves
            act_ref[r0:r0 + rows, c * ck:(c + 1) * ck] = (a * jax.nn.sigmoid(a) * gate).astype(BF16)

    up_proj(0)
    for c in range(n_chunks):
        if c + 1 < n_chunks:
            up_proj(c + 1)
        conv_gate(c)
    o_ref[0] = x + mod_ref[0, 5:6, :] * _dot(act_ref[...], down_ref[...])


def _conv_ffn(x, mod, g, up, dw, db, down, *, layer, mod_row=None):
    b, l, d = x.shape
    tm = min(TM_FFN, l)
    ffn = down.shape[1]
    hb = tm // FFN_HALO
    nhb = l // FFN_HALO
    return pl.pallas_call(
        functools.partial(_ffn_kernel, tm=tm, ffn=ffn, rows=min(FFN_ROWS, tm)),
        grid=(b, l // tm),
        in_specs=[
            pl.BlockSpec((1, tm, d), lambda bi, i: (bi, i, 0)),
            pl.BlockSpec((1, FFN_HALO, d), lambda bi, i: (bi, jnp.maximum(i * hb - 1, 0), 0)),
            pl.BlockSpec((1, FFN_HALO, d),
                         lambda bi, i: (bi, jnp.minimum((i + 1) * hb, nhb - 1), 0)),
            _mod_spec(mod, layer, mod_row),
            _layer_spec(g, layer), _layer_spec(up, layer), _layer_spec(dw, layer), _layer_spec(db, layer),
            _layer_spec(down, layer),
        ],
        out_specs=pl.BlockSpec((1, tm, d), lambda bi, i: (bi, i, 0)),
        out_shape=jax.ShapeDtypeStruct((b, l, d), F32),
        scratch_shapes=[
            pltpu.VMEM((tm + 2 * FFN_HALO, d), BF16),
            pltpu.VMEM((2, 2, tm + 2 * FFN_HALO, FFN_CHUNK), F32),
            pltpu.VMEM((tm, ffn), BF16),
        ],
        compiler_params=_cparams(2),
        name="conv_ffn",
    )(x, x, x, mod, g, up, dw, db, down)


def _rope_tables(seq):
    quarter = HEAD_DIM // 4
    inv = np.power(ROPE_BASE, -np.arange(quarter, dtype=np.float64) / quarter)
    t = np.arange(seq)
    rows = (t // GRID_W)[:, None] * inv[None, :]
    cols = (t % GRID_W)[:, None] * inv[None, :]
    cos = np.concatenate([np.cos(rows)] * 2 + [np.cos(cols)] * 2, axis=1)
    sin = np.concatenate([-np.sin(rows), np.sin(rows), -np.sin(cols), np.sin(cols)], axis=1)
    reps = LANES // HEAD_DIM
    return (np.tile(cos, (1, reps)).astype(np.float32), np.tile(sin, (1, reps)).astype(np.float32))


def kernel(x, c, ctx, c_ctx, ada_w, ada_b, norm1_g, norm2_g, w_in, gate_b, conv_dw_w, conv_dw_b, conv_ln_g, conv_ln_b, conv_out, q_norm_g, k_norm_g, attn_sink, attn_out, sgu_ln_g, sgu_ln_b, sgu_w, sgu_b, sgu_out, w_o, ffn_up, ffn_dw_w, ffn_dw_b, ffn_down):
    bsz, seq, d = x.shape
    depth = ada_w.shape[0]
    n_ctx = ctx.shape[1]

    n_rows = -(-(bsz + 1) // SUBLANES) * SUBLANES
    cond = jnp.concatenate([c, c_ctx[None], jnp.zeros((n_rows - bsz - 1, d), F32)], axis=0)
    mods = _modulation(cond, ada_w, ada_b).reshape(depth, n_rows, 6, d)

    cos, sin = _rope_tables(seq)
    no_rope = np.zeros((n_ctx, LANES), np.float32)
    head_id = np.arange(LANES) // HEAD_DIM
    e = (head_id[:, None] == head_id[None, :]).astype(np.float32)
    e = jnp.asarray(np.concatenate([e, e], axis=0), BF16)

    nab = 2 * CONV_DIM
    n_in1 = nab + (N_HEADS + 2 * N_KV_HEADS) * HEAD_DIM
    groups = CONV_DIM // LANES
    assert w_in.shape[2] - n_in1 == (1 + N_BRANCH) * d and 2 * SGU_DIM == d

    def row(p):
        return p[:, None, :]

    win = w_in.astype(BF16)
    wab = (win[:, :, :nab].reshape(depth, d, 2, groups, LANES)
           .transpose(0, 1, 3, 2, 4).reshape(depth, d, nab))
    gb = gate_b[:, :, None, :]
    gqk = row(jnp.concatenate([jnp.tile(q_norm_g * (LOG2E * HEAD_DIM ** -0.5), (1, N_HEADS)),
                               jnp.tile(k_norm_g, (1, N_KV_HEADS))], axis=1))
    sink = attn_sink * LOG2E
    sw = sgu_w.reshape(depth, SGU_GROUPS // 2, 2, SGU_CHUNK, SGU_CHUNK)
    sw = jnp.concatenate([sw[:, :, 0], sw[:, :, 1]], axis=-1).astype(BF16)
    sb = jnp.repeat(jnp.swapaxes(sgu_b, 1, 2), SGU_DIM // SGU_GROUPS, axis=2)
    g1 = row(norm1_g)
    g2 = row(norm2_g)
    merge_w = (wab, win, gb, conv_dw_w, row(conv_dw_b), row(conv_ln_g), row(conv_ln_b),
               conv_out.astype(BF16), attn_out.astype(BF16), row(sgu_ln_g), row(sgu_ln_b), sw, sb,
               sgu_out.astype(BF16), w_o.astype(BF16))
    ffn_w = (ffn_up.astype(BF16), ffn_dw_w, row(ffn_dw_b), ffn_down.astype(BF16))

    for l in range(depth):
        last = l == depth - 1
        qc, kc, vc = _in_proj(ctx, mods, g1, win, e, gqk, no_rope, no_rope, rope=False, layer=l,
                              mod_row=bsz)
        qx, kx, vx = _in_proj(x, mods, g1, win, e, gqk, cos, sin, rope=True, layer=l)
        ax = _attention(sink, qx, kc, vc, kx, vx, layer=l)
        x_mid = _merge(x, mods, g1, ax, *merge_w, layer=l)
        x_new = _conv_ffn(x_mid, mods, g2, *ffn_w, layer=l)
        if not last:
            ac = _attention(sink, qc, kc, vc, layer=l)
            c_mid = _merge(ctx, mods, g1, ac, *merge_w, layer=l, mod_row=bsz)
            ctx = _conv_ffn(c_mid, mods, g2, *ffn_w, layer=l, mod_row=bsz)
        x = x_new
    return x
```

```python
import functools

import jax
import jax.numpy as jnp
import numpy as np
from jax import lax
from jax.experimental import pallas as pl
from jax.experimental.pallas import tpu as pltpu

F32 = jnp.float32
BF16 = jnp.bfloat16

EPS = 1e-6
NEG_INF = -1e30
ROPE_BASE = 10000.0
LOG2E = 1.4426950408889634
GRID_W = 64
HEAD_DIM = 64
N_HEADS = 8
N_KV_HEADS = 2
WINDOW = 128
CONV_DIM = 512
CONV_WIDTH = 31
SGU_DIM = 512
SGU_GROUPS = 8
SGU_CHUNK = 128
FFN_CONV_WIDTH = 3
N_BRANCH = 3

LANES = 128
SUBLANES = 8
CONV_HALO = 16
FFN_HALO = 8
FFN_CHUNK = 256
FFN_ROWS = 256
CONV_ROWS = 64
VMEM_LIMIT = 56 * 1024 * 1024
MOD_TN = 1536

TM_IN = 1024
TQ_ATTN = 256
TM_MERGE = 512
TM_FFN = 1024


def _cparams(n_axes):
    return pltpu.CompilerParams(
        dimension_semantics=("parallel",) * n_axes, vmem_limit_bytes=VMEM_LIMIT)


def _dot(a, b):
    return jnp.dot(a, b, preferred_element_type=F32)


def _dot_nt(a, b):
    return lax.dot_general(a, b, (((1,), (1,)), ((), ())), preferred_element_type=F32)


def _norm_mod(x, g, shift, scale):
    ms = jnp.mean(x * x, axis=-1, keepdims=True)
    return x * lax.rsqrt(ms + EPS) * (g * (1.0 + scale)) + shift


def _layer_norm(x, g, b):
    mu = jnp.mean(x, axis=-1, keepdims=True)
    xc = x - mu
    var = jnp.mean(xc * xc, axis=-1, keepdims=True)
    return xc * lax.rsqrt(var + EPS) * g + b


def _low_half_lanes():
    return lax.broadcasted_iota(jnp.int32, (1, LANES), 1) < (LANES // 2)


def _full_spec(a):
    return pl.BlockSpec(a.shape, lambda bi, i: (0,) * a.ndim)


def _mod_spec(mods, layer, row):
    blk = (None, 1) + mods.shape[2:]
    if row is None:
        return pl.BlockSpec(blk, lambda bi, i: (layer, bi, 0, 0))
    return pl.BlockSpec(blk, lambda bi, i: (layer, row, 0, 0))


def _layer_spec(a, layer):
    return pl.BlockSpec((None,) + a.shape[1:], lambda bi, i: (layer,) + (0,) * (a.ndim - 1),
                        pipeline_mode=pl.Buffered(1))


def _mod_kernel(c_ref, w_ref, b_ref, o_ref):
    c = c_ref[...]
    s = c * jax.nn.sigmoid(c)
    w = w_ref[0]
    s_hi = s.astype(BF16)
    s_lo = (s - s_hi.astype(F32)).astype(BF16)
    w_hi = w.astype(BF16)
    w_lo = (w - w_hi.astype(F32)).astype(BF16)
    o_ref[0] = _dot(s_hi, w_hi) + _dot(s_lo, w_hi) + _dot(s_hi, w_lo) + b_ref[0]


def _modulation(cond, ada_w, ada_b):
    depth, d, n6 = ada_w.shape
    r = cond.shape[0]
    tn = MOD_TN
    return pl.pallas_call(
        _mod_kernel,
        grid=(depth, n6 // tn),
        in_specs=[
            pl.BlockSpec((r, d), lambda l, j: (0, 0)),
            pl.BlockSpec((1, d, tn), lambda l, j: (l, 0, j)),
            pl.BlockSpec((1, 1, tn), lambda l, j: (l, 0, j)),
        ],
        out_specs=pl.BlockSpec((1, r, tn), lambda l, j: (l, 0, j)),
        out_shape=jax.ShapeDtypeStruct((depth, r, n6), F32),
        compiler_params=_cparams(2),
        name="modulation",
    )(cond, ada_w, ada_b.reshape(depth, 1, n6))


def _in_kernel(x_ref, mod_ref, g_ref, w_ref, e_ref, gqk_ref, cos_ref, sin_ref,
               q_ref, k_ref, v_ref, *, rope):
    x = x_ref[0]
    h = _norm_mod(x, g_ref[...], mod_ref[0, 0:1, :], mod_ref[0, 1:2, :]).astype(BF16)
    qkv = _dot(h, w_ref[:, :(N_HEADS + 2 * N_KV_HEADS) * HEAD_DIM])
    nq = N_HEADS * HEAD_DIM
    nqk = nq + N_KV_HEADS * HEAD_DIM
    e = e_ref[...]
    if rope:
        lane = lax.broadcasted_iota(jnp.int32, (1, LANES), 1)
        first_half = (lane % (HEAD_DIM // 2)) < (HEAD_DIM // 4)
        cos = cos_ref[...]
        sin = sin_ref[...]
    outs = []
    for j in range(nqk // LANES):
        t = qkv[:, j * LANES:(j + 1) * LANES]
        sq = t * t
        hi = sq.astype(BF16)
        lo = (sq - hi.astype(F32)).astype(BF16)
        ss = _dot(jnp.concatenate([hi, lo], axis=1), e)
        tn = t * lax.rsqrt(ss * (1.0 / HEAD_DIM) + EPS) * gqk_ref[:, j * LANES:(j + 1) * LANES]
        if rope:
            quarter = HEAD_DIM // 4
            partner = jnp.where(first_half,
                                pltpu.roll(tn, LANES - quarter, 1),
                                pltpu.roll(tn, quarter, 1))
            tn = tn * cos + partner * sin
        outs.append(tn)
    q_ref[0] = jnp.concatenate(outs[:nq // LANES], axis=1).astype(BF16)
    low = _low_half_lanes()
    kk = outs[nq // LANES]
    kr = pltpu.roll(kk, LANES // 2, 1)
    k_ref[0] = jnp.concatenate([jnp.where(low, kk, kr), jnp.where(low, kr, kk)], axis=1).astype(BF16)
    vv = qkv[:, nqk:]
    vr = pltpu.roll(vv, LANES // 2, 1)
    v_ref[0] = jnp.concatenate([jnp.where(low, vv, 0.0), jnp.where(low, 0.0, vr),
                                jnp.where(low, vr, 0.0), jnp.where(low, 0.0, vv)], axis=1).astype(BF16)


def _in_proj(x, mod, g, w, e, gqk, cos, sin, *, rope, layer, mod_row=None):
    b, l, d = x.shape
    tm = min(TM_IN, l)
    nq = N_HEADS * HEAD_DIM
    nk = 2 * N_KV_HEADS * HEAD_DIM
    nv = 4 * N_KV_HEADS * HEAD_DIM
    assert (N_HEADS + 2 * N_KV_HEADS) * HEAD_DIM <= 2 * CONV_DIM
    return pl.pallas_call(
        functools.partial(_in_kernel, rope=rope),
        grid=(b, l // tm),
        in_specs=[
            pl.BlockSpec((1, tm, d), lambda bi, i: (bi, i, 0)),
            _mod_spec(mod, layer, mod_row),
            _layer_spec(g, layer),
            pl.BlockSpec((None, d, 2 * CONV_DIM), lambda bi, i: (layer, 0, 1),
                         pipeline_mode=pl.Buffered(1)),
            _full_spec(e), _layer_spec(gqk, layer),
            pl.BlockSpec((tm, LANES), lambda bi, i: (i, 0)),
            pl.BlockSpec((tm, LANES), lambda bi, i: (i, 0)),
        ],
        out_specs=[
            pl.BlockSpec((1, tm, nq), lambda bi, i: (bi, i, 0)),
            pl.BlockSpec((1, tm, nk), lambda bi, i: (bi, i, 0)),
            pl.BlockSpec((1, tm, nv), lambda bi, i: (bi, i, 0)),
        ],
        out_shape=[
            jax.ShapeDtypeStruct((b, l, nq), BF16),
            jax.ShapeDtypeStruct((b, l, nk), BF16),
            jax.ShapeDtypeStruct((b, l, nv), BF16),
        ],
        compiler_params=_cparams(2),
        name="in_proj_rope" if rope else "in_proj_ctx",
    )(x, mod, g, w, e, gqk, cos, sin)


def _attn_kernel(sink_ref, q_ref, kc_ref, vc_ref, *rest, local, tq, seq, layer):
    if local:
        k_ref, v_ref, o_ref, bias_ref = rest
        i = pl.program_id(1)
        width = tq + 2 * WINDOW
        t0 = i * tq
        start = pl.multiple_of(jnp.clip(t0 - WINDOW, 0, seq - width), WINDOW)
        qpos = t0 + lax.broadcasted_iota(jnp.int32, (tq, 1), 0)
        kpos = start + lax.broadcasted_iota(jnp.int32, (1, width), 1)
        bias_ref[...] = jnp.where(jnp.abs(qpos - kpos) <= WINDOW, 0.0, NEG_INF)
    else:
        (o_ref,) = rest
    low = _low_half_lanes()
    grp = N_HEADS // N_KV_HEADS
    n_ctx = kc_ref.shape[1]

    def scores(h):
        hk = h // grp
        qp = q_ref[0, :, (h // 2) * LANES:(h // 2 + 1) * LANES]
        qh = jnp.where(low, qp, 0.0) if h % 2 == 0 else jnp.where(low, 0.0, qp)
        s = _dot_nt(qh, kc_ref[0, :, hk * LANES:(hk + 1) * LANES])
        if local:
            s_loc = _dot_nt(qh, k_ref[0, pl.ds(start, width), hk * LANES:(hk + 1) * LANES])
            s = jnp.concatenate([s, s_loc + bias_ref[...]], axis=1)
        return s

    def softmax(h, s):
        sink = sink_ref[layer, h]
        m = jnp.maximum(jnp.max(s, axis=-1, keepdims=True), sink)
        p = jnp.exp2(s - m)
        den = jnp.sum(p, axis=-1, keepdims=True) + jnp.exp2(sink - m)
        return p.astype(BF16), 1.0 / den

    def values(h, p, inv):
        vs = slice((2 * (h // grp) + h % 2) * LANES, (2 * (h // grp) + h % 2 + 1) * LANES)
        o = _dot(p[:, :n_ctx], vc_ref[0, :, vs])
        if local:
            o = o + _dot(p[:, n_ctx:], v_ref[0, pl.ds(start, width), vs])
        return o * inv

    s = scores(0)
    o_pair = None
    for h in range(N_HEADS):
        s_next = scores(h + 1) if h + 1 < N_HEADS else None
        p, inv = softmax(h, s)
        o = values(h, p, inv)
        if h % 2 == 0:
            o_pair = o
        else:
            o_ref[0, :, (h // 2) * LANES:(h // 2 + 1) * LANES] = (o_pair + o).astype(o_ref.dtype)
        s = s_next


def _attention(sink, q, kc, vc, k=None, v=None, *, layer):
    b, l, nq = q.shape
    lc = kc.shape[1]
    local = k is not None
    tq = min(TQ_ATTN, l)
    if local:
        assert l >= tq + 2 * WINDOW and tq % WINDOW == 0
    in_specs = [
        pl.BlockSpec(memory_space=pltpu.SMEM),
        pl.BlockSpec((1, tq, nq), lambda bi, i: (bi, i, 0)),
        pl.BlockSpec((1, lc, kc.shape[2]), lambda bi, i: (bi, 0, 0)),
        pl.BlockSpec((1, lc, vc.shape[2]), lambda bi, i: (bi, 0, 0)),
    ]
    args = [sink, q, kc, vc]
    if local:
        in_specs += [pl.BlockSpec((1, l, k.shape[2]), lambda bi, i: (bi, 0, 0)),
                     pl.BlockSpec((1, l, v.shape[2]), lambda bi, i: (bi, 0, 0))]
        args += [k, v]
    return pl.pallas_call(
        functools.partial(_attn_kernel, local=local, tq=tq, seq=l, layer=layer),
        grid=(b, l // tq),
        in_specs=in_specs,
        out_specs=pl.BlockSpec((1, tq, nq), lambda bi, i: (bi, i, 0)),
        out_shape=jax.ShapeDtypeStruct((b, l, nq), BF16),
        scratch_shapes=[pltpu.VMEM((tq, tq + 2 * WINDOW), F32)] if local else [],
        compiler_params=_cparams(2),
        name="attn_window" if local else "attn_ctx",
    )(*args)


def _merge_kernel(x_ref, xp_ref, xn_ref, mod_ref, g_ref, a_ref,
                  wab_ref, win_ref, gb_ref,
                  cw_ref, cb_ref, clg_ref, clb_ref, cout_ref, aout_ref,
                  slg_ref, slb_ref, sw_ref, sb_ref, sout_ref, wo_ref,
                  o_ref, h_ref, ys_ref, yc_ref, *, tm, rows):
    i = pl.program_id(1)
    n = pl.num_programs(1)
    g = g_ref[...]
    shift = mod_ref[0, 0:1, :]
    scale = mod_ref[0, 1:2, :]
    x = x_ref[0]
    y_attn = _dot(a_ref[0], aout_ref[...])
    h_ref[...] = jnp.concatenate(
        [jnp.where(i > 0, _norm_mod(xp_ref[0], g, shift, scale), 0.0),
         _norm_mod(x, g, shift, scale),
         jnp.where(i < n - 1, _norm_mod(xn_ref[0], g, shift, scale), 0.0)], axis=0).astype(BF16)
    ext = tm + 2 * CONV_HALO
    base = CONV_HALO - CONV_WIDTH // 2
    span = (base + CONV_WIDTH - 1) // SUBLANES * SUBLANES
    n_groups = CONV_DIM // LANES

    def glu_proj(c):
        ab = _dot(h_ref[...], wab_ref[:, 2 * c * LANES:2 * (c + 1) * LANES])
        y = ab[:, :LANES] * jax.nn.sigmoid(ab[:, LANES:])
        ys_ref[c % 2, 0] = y
        for r in range(1, SUBLANES):
            ys_ref[c % 2, r] = pltpu.roll(y, ext - r, 0)

    def taps(c):
        cols = slice(c * LANES, (c + 1) * LANES)
        for r0 in range(0, tm, rows):
            acc = jnp.broadcast_to(cb_ref[:, cols], (rows, LANES))
            for r in range(SUBLANES):
                slab = ys_ref[c % 2, r, r0:r0 + rows + span, :]
                for t in range(CONV_WIDTH):
                    if (base + t) % SUBLANES == r:
                        lo = base + t - r
                        acc = acc + cw_ref[t:t + 1, cols] * slab[lo:lo + rows]
            yc_ref[r0:r0 + rows, cols] = acc

    def h_main():
        return h_ref[CONV_HALO:CONV_HALO + tm, :]

    d = x.shape[-1]
    rest = 2 * CONV_DIM + (N_HEADS + 2 * N_KV_HEADS) * HEAD_DIM

    def h_proj(k):
        out = _dot(h_main(), win_ref[:, rest + k * d:rest + (k + 1) * d])
        return out if k == 0 else out + gb_ref[k - 1]

    n_pre = 1 + N_BRANCH
    pre = []
    glu_proj(0)
    for c in range(n_groups):
        if c + 1 < n_groups:
            glu_proj(c + 1)
        if len(pre) < n_pre:
            pre.append(h_proj(len(pre)))
        taps(c)
    while len(pre) < n_pre:
        pre.append(h_proj(len(pre)))
    uv, gates = pre[0], pre[1:]

    u = jax.nn.gelu(uv[:, :SGU_DIM])
    v = _layer_norm(jax.nn.gelu(uv[:, SGU_DIM:]), slg_ref[...], slb_ref[...])
    low = _low_half_lanes()
    mixed = []
    for c in range(tm // SGU_CHUNK):
        cols = []
        for j in range(SGU_DIM // LANES):
            vj = v[c * SGU_CHUNK:(c + 1) * SGU_CHUNK, j * LANES:(j + 1) * LANES]
            stacked = jnp.concatenate(
                [jnp.where(low, vj, 0.0), jnp.where(low, 0.0, vj)], axis=0).astype(BF16)
            cols.append(_dot(sw_ref[j], stacked))
        mixed.append(jnp.concatenate(cols, axis=1) + sb_ref[...])
    mixed = jnp.concatenate(mixed, axis=0) if len(mixed) > 1 else mixed[0]
    y_sgu = _dot((u * mixed).astype(BF16), sout_ref[...])

    yc = _layer_norm(yc_ref[...], clg_ref[...], clb_ref[...])
    yc = yc * jax.nn.sigmoid(yc)
    y_conv = _dot(yc.astype(BF16), cout_ref[...])
    m = (jax.nn.sigmoid(gates[0]) * y_conv + jax.nn.sigmoid(gates[1]) * y_attn
         + jax.nn.sigmoid(gates[2]) * y_sgu)

    o_ref[0] = x + mod_ref[0, 2:3, :] * _dot(m.astype(BF16), wo_ref[...])


def _merge(x, mod, g, attn, wab, win, gb, cw, cb, clg, clb, cout, aout,
           slg, slb, sw, sb, sout, wo, *, layer, mod_row=None):
    b, l, d = x.shape
    tm = min(TM_MERGE, l)
    hb = tm // CONV_HALO
    nhb = l // CONV_HALO
    weights = (wab, win, gb, cw, cb, clg, clb, cout, aout, slg, slb, sw, sb, sout, wo)
    return pl.pallas_call(
        functools.partial(_merge_kernel, tm=tm, rows=CONV_ROWS),
        grid=(b, l // tm),
        in_specs=[
            pl.BlockSpec((1, tm, d), lambda bi, i: (bi, i, 0)),
            pl.BlockSpec((1, CONV_HALO, d), lambda bi, i: (bi, jnp.maximum(i * hb - 1, 0), 0)),
            pl.BlockSpec((1, CONV_HALO, d),
                         lambda bi, i: (bi, jnp.minimum((i + 1) * hb, nhb - 1), 0)),
            _mod_spec(mod, layer, mod_row),
            _layer_spec(g, layer),
            pl.BlockSpec((1, tm, attn.shape[-1]), lambda bi, i: (bi, i, 0)),
        ] + [_layer_spec(a, layer) for a in weights],
        out_specs=pl.BlockSpec((1, tm, d), lambda bi, i: (bi, i, 0)),
        out_shape=jax.ShapeDtypeStruct((b, l, d), F32),
        scratch_shapes=[
            pltpu.VMEM((tm + 2 * CONV_HALO, d), BF16),
            pltpu.VMEM((2, SUBLANES, tm + 2 * CONV_HALO, LANES), F32),
            pltpu.VMEM((tm, CONV_DIM), F32),
        ],
        compiler_params=_cparams(2),
        name="mixer_merge",
    )(x, x, x, mod, g, attn, *weights)


def _ffn_kernel(x_ref, xp_ref, xn_ref, mod_ref, g_ref, up_ref, dw_ref, db_ref, down_ref,
                o_ref, h_ref, z_ref, act_ref, *, tm, ffn, rows):
    i = pl.program_id(1)
    n = pl.num_programs(1)
    g = g_ref[...]
    shift = mod_ref[0, 3:4, :]
    scale = mod_ref[0, 4:5, :]
    x = x_ref[0]
    h_ref[...] = jnp.concatenate(
        [jnp.where(i > 0, _norm_mod(xp_ref[0], g, shift, scale), 0.0),
         _norm_mod(x, g, shift, scale),
         jnp.where(i < n - 1, _norm_mod(xn_ref[0], g, shift, scale), 0.0)], axis=0).astype(BF16)
    ck = FFN_CHUNK
    n_chunks = ffn // ck
    ext = rows + 2 * FFN_HALO

    def offsets(c):
        return (c * ck, ffn + c * ck)

    def up_proj(c):
        for half, off in enumerate(offsets(c)):
            z_ref[c % 2, half] = _dot(h_ref[...], up_ref[:, off:off + ck])

    def rows_of(ref, t, off):
        return jnp.concatenate([ref[t * SUBLANES:(t + 1) * SUBLANES, off:off + ck]] * (rows // SUBLANES),
                               axis=0)

    def conv_gate(c):
        for r0 in range(0, tm, rows):
            halves = []
            for half, off in enumerate(offsets(c)):
                z = z_ref[c % 2, half, r0:r0 + ext, :]
                halves.append(
                    rows_of(db_ref, 0, off)
                    + rows_of(dw_ref, 0, off) * pltpu.roll(z, 1, 0)[FFN_HALO:FFN_HALO + rows]
                    + rows_of(dw_ref, 1, off) * z[FFN_HALO:FFN_HALO + rows]
                    + rows_of(dw_ref, 2, off) * pltpu.roll(z, ext - 1, 0)[FFN_HALO:FFN_HALO + rows])
            a, gate = halves
            act_ref[r0:r0 + rows, c * ck:(c + 1) * ck] = (a * jax.nn.sigmoid(a) * gate).astype(BF16)

    up_proj(0)
    for c in range(n_chunks):
        if c + 1 < n_chunks:
            up_proj(c + 1)
        conv_gate(c)
    o_ref[0] = x + mod_ref[0, 5:6, :] * _dot(act_ref[...], down_ref[...])


def _conv_ffn(x, mod, g, up, dw, db, down, *, layer, mod_row=None):
    b, l, d = x.shape
    tm = min(TM_FFN, l)
    ffn = down.shape[1]
    hb = tm // FFN_HALO
    nhb = l // FFN_HALO
    return pl.pallas_call(
        functools.partial(_ffn_kernel, tm=tm, ffn=ffn, rows=min(FFN_ROWS, tm)),
        grid=(b, l // tm),
        in_specs=[
            pl.BlockSpec((1, tm, d), lambda bi, i: (bi, i, 0)),
            pl.BlockSpec((1, FFN_HALO, d), lambda bi, i: (bi, jnp.maximum(i * hb - 1, 0), 0)),
            pl.BlockSpec((1, FFN_HALO, d),
                         lambda bi, i: (bi, jnp.minimum((i + 1) * hb, nhb - 1), 0)),
            _mod_spec(mod, layer, mod_row),
            _layer_spec(g, layer), _layer_spec(up, layer), _layer_spec(dw, layer), _layer_spec(db, layer),
            _layer_spec(down, layer),
        ],
        out_specs=pl.BlockSpec((1, tm, d), lambda bi, i: (bi, i, 0)),
        out_shape=jax.ShapeDtypeStruct((b, l, d), F32),
        scratch_shapes=[
            pltpu.VMEM((tm + 2 * FFN_HALO, d), BF16),
            pltpu.VMEM((2, 2, tm + 2 * FFN_HALO, FFN_CHUNK), F32),
            pltpu.VMEM((tm, ffn), BF16),
        ],
        compiler_params=_cparams(2),
        name="conv_ffn",
    )(x, x, x, mod, g, up, dw, db, down)


def _rope_tables(seq):
    quarter = HEAD_DIM // 4
    inv = np.power(ROPE_BASE, -np.arange(quarter, dtype=np.float64) / quarter)
    t = np.arange(seq)
    rows = (t // GRID_W)[:, None] * inv[None, :]
    cols = (t % GRID_W)[:, None] * inv[None, :]
    cos = np.concatenate([np.cos(rows)] * 2 + [np.cos(cols)] * 2, axis=1)
    sin = np.concatenate([-np.sin(rows), np.sin(rows), -np.sin(cols), np.sin(cols)], axis=1)
    reps = LANES // HEAD_DIM
    return (np.tile(cos, (1, reps)).astype(np.float32), np.tile(sin, (1, reps)).astype(np.float32))


def kernel(x, c, ctx, c_ctx, ada_w, ada_b, norm1_g, norm2_g, w_in, gate_b, conv_dw_w, conv_dw_b, conv_ln_g, conv_ln_b, conv_out, q_norm_g, k_norm_g, attn_sink, attn_out, sgu_ln_g, sgu_ln_b, sgu_w, sgu_b, sgu_out, w_o, ffn_up, ffn_dw_w, ffn_dw_b, ffn_down):
    bsz, seq, d = x.shape
    depth = ada_w.shape[0]
    n_ctx = ctx.shape[1]

    n_rows = -(-(bsz + 1) // SUBLANES) * SUBLANES
    cond = jnp.concatenate([c, c_ctx[None], jnp.zeros((n_rows - bsz - 1, d), F32)], axis=0)
    mods = _modulation(cond, ada_w, ada_b).reshape(depth, n_rows, 6, d)

    cos, sin = _rope_tables(seq)
    no_rope = np.zeros((n_ctx, LANES), np.float32)
    head_id = np.arange(LANES) // HEAD_DIM
    e = (head_id[:, None] == head_id[None, :]).astype(np.float32)
    e = jnp.asarray(np.concatenate([e, e], axis=0), BF16)

    nab = 2 * CONV_DIM
    n_in1 = nab + (N_HEADS + 2 * N_KV_HEADS) * HEAD_DIM
    groups = CONV_DIM // LANES
    assert w_in.shape[2] - n_in1 == (1 + N_BRANCH) * d and 2 * SGU_DIM == d

    def row(p):
        return p[:, None, :]

    win = w_in.astype(BF16)
    wab = (win[:, :, :nab].reshape(depth, d, 2, groups, LANES)
           .transpose(0, 1, 3, 2, 4).reshape(depth, d, nab))
    gb = gate_b[:, :, None, :]
    gqk = row(jnp.concatenate([jnp.tile(q_norm_g * (LOG2E * HEAD_DIM ** -0.5), (1, N_HEADS)),
                               jnp.tile(k_norm_g, (1, N_KV_HEADS))], axis=1))
    sink = attn_sink * LOG2E
    sw = sgu_w.reshape(depth, SGU_GROUPS // 2, 2, SGU_CHUNK, SGU_CHUNK)
    sw = jnp.concatenate([sw[:, :, 0], sw[:, :, 1]], axis=-1).astype(BF16)
    sb = jnp.repeat(jnp.swapaxes(sgu_b, 1, 2), SGU_DIM // SGU_GROUPS, axis=2)
    g1 = row(norm1_g)
    g2 = row(norm2_g)
    merge_w = (wab, win, gb, conv_dw_w, row(conv_dw_b), row(conv_ln_g), row(conv_ln_b),
               conv_out.astype(BF16), attn_out.astype(BF16), row(sgu_ln_g), row(sgu_ln_b), sw, sb,
               sgu_out.astype(BF16), w_o.astype(BF16))
    ffn_w = (ffn_up.astype(BF16), jnp.repeat(ffn_dw_w, SUBLANES, axis=1),
             jnp.repeat(row(ffn_dw_b), SUBLANES, axis=1), ffn_down.astype(BF16))

    for l in range(depth):
        last = l == depth - 1
        qc, kc, vc = _in_proj(ctx, mods, g1, win, e, gqk, no_rope, no_rope, rope=False, layer=l,
                              mod_row=bsz)
        qx, kx, vx = _in_proj(x, mods, g1, win, e, gqk, cos, sin, rope=True, layer=l)
        ax = _attention(sink, qx, kc, vc, kx, vx, layer=l)
        x_mid = _merge(x, mods, g1, ax, *merge_w, layer=l)
        x_new = _conv_ffn(x_mid, mods, g2, *ffn_w, layer=l)
        if not last:
            ac = _attention(sink, qc, kc, vc, layer=l)
            c_mid = _merge(ctx, mods, g1, ac, *merge_w, layer=l, mod_row=bsz)
            ctx = _conv_ffn(c_mid, mods, g2, *ffn_w, layer=l, mod_row=bsz)
        x = x_new
    return x
```

```python
import functools

import jax
import jax.numpy as jnp
import numpy as np
from jax import lax
from jax.experimental import pallas as pl
from jax.experimental.pallas import tpu as pltpu

F32 = jnp.float32
BF16 = jnp.bfloat16

EPS = 1e-6
NEG_INF = -1e30
ROPE_BASE = 10000.0
LOG2E = 1.4426950408889634
GRID_W = 64
HEAD_DIM = 64
N_HEADS = 8
N_KV_HEADS = 2
WINDOW = 128
CONV_DIM = 512
CONV_WIDTH = 31
SGU_DIM = 512
SGU_GROUPS = 8
SGU_CHUNK = 128
FFN_CONV_WIDTH = 3
N_BRANCH = 3

LANES = 128
SUBLANES = 8
CONV_HALO = 16
FFN_HALO = 8
FFN_CHUNK = 256
FFN_ROWS = 256
CONV_ROWS = 64
VMEM_LIMIT = 56 * 1024 * 1024
MOD_TN = 1536

TM_IN = 1024
TQ_ATTN = 256
TM_MERGE = 512
TM_FFN = 1024


def _cparams(n_axes):
    return pltpu.CompilerParams(
        dimension_semantics=("parallel",) * n_axes, vmem_limit_bytes=VMEM_LIMIT)


def _dot(a, b):
    return jnp.dot(a, b, preferred_element_type=F32)


def _dot_nt(a, b):
    return lax.dot_general(a, b, (((1,), (1,)), ((), ())), preferred_element_type=F32)


def _norm_mod(x, g, shift, scale):
    ms = jnp.mean(x * x, axis=-1, keepdims=True)
    return x * lax.rsqrt(ms + EPS) * (g * (1.0 + scale)) + shift


def _layer_norm(x, g, b):
    mu = jnp.mean(x, axis=-1, keepdims=True)
    xc = x - mu
    var = jnp.mean(xc * xc, axis=-1, keepdims=True)
    return xc * lax.rsqrt(var + EPS) * g + b


def _low_half_lanes():
    return lax.broadcasted_iota(jnp.int32, (1, LANES), 1) < (LANES // 2)


def _full_spec(a):
    return pl.BlockSpec(a.shape, lambda bi, i: (0,) * a.ndim)


def _mod_spec(mods, layer, row):
    blk = (None, 1) + mods.shape[2:]
    if row is None:
        return pl.BlockSpec(blk, lambda bi, i: (layer, bi, 0, 0))
    return pl.BlockSpec(blk, lambda bi, i: (layer, row, 0, 0))


def _param_spec(a, layer):
    return _full_spec(a) if a.ndim == 2 else _layer_spec(a, layer)


def _layer_spec(a, layer):
    return pl.BlockSpec((None,) + a.shape[1:], lambda bi, i: (layer,) + (0,) * (a.ndim - 1),
                        pipeline_mode=pl.Buffered(1))


def _mod_kernel(c_ref, w_ref, b_ref, o_ref):
    c = c_ref[...]
    s = c * jax.nn.sigmoid(c)
    w = w_ref[0]
    s_hi = s.astype(BF16)
    s_lo = (s - s_hi.astype(F32)).astype(BF16)
    w_hi = w.astype(BF16)
    w_lo = (w - w_hi.astype(F32)).astype(BF16)
    o_ref[0] = _dot(s_hi, w_hi) + _dot(s_lo, w_hi) + _dot(s_hi, w_lo) + b_ref[0]


def _modulation(cond, ada_w, ada_b):
    depth, d, n6 = ada_w.shape
    r = cond.shape[0]
    tn = MOD_TN
    return pl.pallas_call(
        _mod_kernel,
        grid=(depth, n6 // tn),
        in_specs=[
            pl.BlockSpec((r, d), lambda l, j: (0, 0)),
            pl.BlockSpec((1, d, tn), lambda l, j: (l, 0, j)),
            pl.BlockSpec((1, 1, tn), lambda l, j: (l, 0, j)),
        ],
        out_specs=pl.BlockSpec((1, r, tn), lambda l, j: (l, 0, j)),
        out_shape=jax.ShapeDtypeStruct((depth, r, n6), F32),
        compiler_params=_cparams(2),
        name="modulation",
    )(cond, ada_w, ada_b.reshape(depth, 1, n6))


def _in_kernel(x_ref, mod_ref, g_ref, w_ref, e_ref, gqk_ref, cos_ref, sin_ref,
               q_ref, k_ref, v_ref, *, rope, layer):
    x = x_ref[0]
    h = _norm_mod(x, g_ref[layer:layer + 1, :], mod_ref[0, 0:1, :], mod_ref[0, 1:2, :]).astype(BF16)
    qkv = _dot(h, w_ref[:, :(N_HEADS + 2 * N_KV_HEADS) * HEAD_DIM])
    nq = N_HEADS * HEAD_DIM
    nqk = nq + N_KV_HEADS * HEAD_DIM
    e = e_ref[...]
    if rope:
        lane = lax.broadcasted_iota(jnp.int32, (1, LANES), 1)
        first_half = (lane % (HEAD_DIM // 2)) < (HEAD_DIM // 4)
        cos = cos_ref[...]
        sin = sin_ref[...]
    outs = []
    for j in range(nqk // LANES):
        t = qkv[:, j * LANES:(j + 1) * LANES]
        sq = t * t
        hi = sq.astype(BF16)
        lo = (sq - hi.astype(F32)).astype(BF16)
        ss = _dot(jnp.concatenate([hi, lo], axis=1), e)
        tn = t * lax.rsqrt(ss * (1.0 / HEAD_DIM) + EPS) * gqk_ref[layer:layer + 1, j * LANES:(j + 1) * LANES]
        if rope:
            quarter = HEAD_DIM // 4
            partner = jnp.where(first_half,
                                pltpu.roll(tn, LANES - quarter, 1),
                                pltpu.roll(tn, quarter, 1))
            tn = tn * cos + partner * sin
        outs.append(tn)
    q_ref[0] = jnp.concatenate(outs[:nq // LANES], axis=1).astype(BF16)
    low = _low_half_lanes()
    kk = outs[nq // LANES]
    kr = pltpu.roll(kk, LANES // 2, 1)
    k_ref[0] = jnp.concatenate([jnp.where(low, kk, kr), jnp.where(low, kr, kk)], axis=1).astype(BF16)
    vv = qkv[:, nqk:]
    vr = pltpu.roll(vv, LANES // 2, 1)
    v_ref[0] = jnp.concatenate([jnp.where(low, vv, 0.0), jnp.where(low, 0.0, vr),
                                jnp.where(low, vr, 0.0), jnp.where(low, 0.0, vv)], axis=1).astype(BF16)


def _in_proj(x, mod, g, w, e, gqk, cos, sin, *, rope, layer, mod_row=None):
    b, l, d = x.shape
    tm = min(TM_IN, l)
    nq = N_HEADS * HEAD_DIM
    nk = 2 * N_KV_HEADS * HEAD_DIM
    nv = 4 * N_KV_HEADS * HEAD_DIM
    assert (N_HEADS + 2 * N_KV_HEADS) * HEAD_DIM <= 2 * CONV_DIM
    return pl.pallas_call(
        functools.partial(_in_kernel, rope=rope, layer=layer),
        grid=(b, l // tm),
        in_specs=[
            pl.BlockSpec((1, tm, d), lambda bi, i: (bi, i, 0)),
            _mod_spec(mod, layer, mod_row),
            _param_spec(g, layer),
            pl.BlockSpec((None, d, 2 * CONV_DIM), lambda bi, i: (layer, 0, 1),
                         pipeline_mode=pl.Buffered(1)),
            _full_spec(e), _param_spec(gqk, layer),
            pl.BlockSpec((tm, LANES), lambda bi, i: (i, 0)),
            pl.BlockSpec((tm, LANES), lambda bi, i: (i, 0)),
        ],
        out_specs=[
            pl.BlockSpec((1, tm, nq), lambda bi, i: (bi, i, 0)),
            pl.BlockSpec((1, tm, nk), lambda bi, i: (bi, i, 0)),
            pl.BlockSpec((1, tm, nv), lambda bi, i: (bi, i, 0)),
        ],
        out_shape=[
            jax.ShapeDtypeStruct((b, l, nq), BF16),
            jax.ShapeDtypeStruct((b, l, nk), BF16),
            jax.ShapeDtypeStruct((b, l, nv), BF16),
        ],
        compiler_params=_cparams(2),
        name="in_proj_rope" if rope else "in_proj_ctx",
    )(x, mod, g, w, e, gqk, cos, sin)


def _attn_kernel(sink_ref, q_ref, kc_ref, vc_ref, *rest, local, tq, seq, layer):
    if local:
        k_ref, v_ref, o_ref, bias_ref = rest
        i = pl.program_id(1)
        width = tq + 2 * WINDOW
        t0 = i * tq
        start = pl.multiple_of(jnp.clip(t0 - WINDOW, 0, seq - width), WINDOW)
        qpos = t0 + lax.broadcasted_iota(jnp.int32, (tq, 1), 0)
        kpos = start + lax.broadcasted_iota(jnp.int32, (1, width), 1)
        bias_ref[...] = jnp.where(jnp.abs(qpos - kpos) <= WINDOW, 0.0, NEG_INF)
    else:
        (o_ref,) = rest
    low = _low_half_lanes()
    grp = N_HEADS // N_KV_HEADS
    n_ctx = kc_ref.shape[1]

    def scores(h):
        hk = h // grp
        qp = q_ref[0, :, (h // 2) * LANES:(h // 2 + 1) * LANES]
        qh = jnp.where(low, qp, 0.0) if h % 2 == 0 else jnp.where(low, 0.0, qp)
        s = _dot_nt(qh, kc_ref[0, :, hk * LANES:(hk + 1) * LANES])
        if local:
            s_loc = _dot_nt(qh, k_ref[0, pl.ds(start, width), hk * LANES:(hk + 1) * LANES])
            s = jnp.concatenate([s, s_loc + bias_ref[...]], axis=1)
        return s

    def softmax(h, s):
        sink = sink_ref[layer, h]
        m = jnp.maximum(jnp.max(s, axis=-1, keepdims=True), sink)
        p = jnp.exp2(s - m)
        den = jnp.sum(p, axis=-1, keepdims=True) + jnp.exp2(sink - m)
        return p.astype(BF16), 1.0 / den

    def values(h, p, inv):
        vs = slice((2 * (h // grp) + h % 2) * LANES, (2 * (h // grp) + h % 2 + 1) * LANES)
        o = _dot(p[:, :n_ctx], vc_ref[0, :, vs])
        if local:
            o = o + _dot(p[:, n_ctx:], v_ref[0, pl.ds(start, width), vs])
        return o * inv

    s = scores(0)
    o_pair = None
    for h in range(N_HEADS):
        s_next = scores(h + 1) if h + 1 < N_HEADS else None
        p, inv = softmax(h, s)
        o = values(h, p, inv)
        if h % 2 == 0:
            o_pair = o
        else:
            o_ref[0, :, (h // 2) * LANES:(h // 2 + 1) * LANES] = (o_pair + o).astype(o_ref.dtype)
        s = s_next


def _attention(sink, q, kc, vc, k=None, v=None, *, layer):
    b, l, nq = q.shape
    lc = kc.shape[1]
    local = k is not None
    tq = min(TQ_ATTN, l)
    if local:
        assert l >= tq + 2 * WINDOW and tq % WINDOW == 0
    in_specs = [
        pl.BlockSpec(memory_space=pltpu.SMEM),
        pl.BlockSpec((1, tq, nq), lambda bi, i: (bi, i, 0)),
        pl.BlockSpec((1, lc, kc.shape[2]), lambda bi, i: (bi, 0, 0)),
        pl.BlockSpec((1, lc, vc.shape[2]), lambda bi, i: (bi, 0, 0)),
    ]
    args = [sink, q, kc, vc]
    if local:
        in_specs += [pl.BlockSpec((1, l, k.shape[2]), lambda bi, i: (bi, 0, 0)),
                     pl.BlockSpec((1, l, v.shape[2]), lambda bi, i: (bi, 0, 0))]
        args += [k, v]
    return pl.pallas_call(
        functools.partial(_attn_kernel, local=local, tq=tq, seq=l, layer=layer),
        grid=(b, l // tq),
        in_specs=in_specs,
        out_specs=pl.BlockSpec((1, tq, nq), lambda bi, i: (bi, i, 0)),
        out_shape=jax.ShapeDtypeStruct((b, l, nq), BF16),
        scratch_shapes=[pltpu.VMEM((tq, tq + 2 * WINDOW), F32)] if local else [],
        compiler_params=_cparams(2),
        name="attn_window" if local else "attn_ctx",
    )(*args)


def _merge_kernel(x_ref, xp_ref, xn_ref, mod_ref, g_ref, a_ref,
                  wab_ref, win_ref, gb_ref,
                  cw_ref, cb_ref, clg_ref, clb_ref, cout_ref, aout_ref,
                  slg_ref, slb_ref, sw_ref, sb_ref, sout_ref, wo_ref,
                  o_ref, h_ref, ys_ref, yc_ref, *, tm, rows, layer):
    i = pl.program_id(1)
    n = pl.num_programs(1)
    g = g_ref[layer:layer + 1, :]
    shift = mod_ref[0, 0:1, :]
    scale = mod_ref[0, 1:2, :]
    x = x_ref[0]
    y_attn = _dot(a_ref[0], aout_ref[...])
    h_ref[...] = jnp.concatenate(
        [jnp.where(i > 0, _norm_mod(xp_ref[0], g, shift, scale), 0.0),
         _norm_mod(x, g, shift, scale),
         jnp.where(i < n - 1, _norm_mod(xn_ref[0], g, shift, scale), 0.0)], axis=0).astype(BF16)
    ext = tm + 2 * CONV_HALO
    base = CONV_HALO - CONV_WIDTH // 2
    span = (base + CONV_WIDTH - 1) // SUBLANES * SUBLANES
    n_groups = CONV_DIM // LANES

    def glu_proj(c):
        ab = _dot(h_ref[...], wab_ref[:, 2 * c * LANES:2 * (c + 1) * LANES])
        y = ab[:, :LANES] * jax.nn.sigmoid(ab[:, LANES:])
        ys_ref[c % 2, 0] = y
        for r in range(1, SUBLANES):
            ys_ref[c % 2, r] = pltpu.roll(y, ext - r, 0)

    def taps(c):
        cols = slice(c * LANES, (c + 1) * LANES)
        for r0 in range(0, tm, rows):
            acc = jnp.broadcast_to(cb_ref[layer:layer + 1, cols], (rows, LANES))
            for r in range(SUBLANES):
                slab = ys_ref[c % 2, r, r0:r0 + rows + span, :]
                for t in range(CONV_WIDTH):
                    if (base + t) % SUBLANES == r:
                        lo = base + t - r
                        acc = acc + cw_ref[t:t + 1, cols] * slab[lo:lo + rows]
            yc_ref[r0:r0 + rows, cols] = acc

    def h_main():
        return h_ref[CONV_HALO:CONV_HALO + tm, :]

    d = x.shape[-1]
    rest = 2 * CONV_DIM + (N_HEADS + 2 * N_KV_HEADS) * HEAD_DIM

    def h_proj(k):
        out = _dot(h_main(), win_ref[:, rest + k * d:rest + (k + 1) * d])
        return out if k == 0 else out + gb_ref[k - 1]

    n_pre = 1 + N_BRANCH
    pre = []
    glu_proj(0)
    for c in range(n_groups):
        if c + 1 < n_groups:
            glu_proj(c + 1)
        if len(pre) < n_pre:
            pre.append(h_proj(len(pre)))
        taps(c)
    while len(pre) < n_pre:
        pre.append(h_proj(len(pre)))
    uv, gates = pre[0], pre[1:]

    u = jax.nn.gelu(uv[:, :SGU_DIM])
    v = _layer_norm(jax.nn.gelu(uv[:, SGU_DIM:]), slg_ref[layer:layer + 1, :], slb_ref[layer:layer + 1, :])
    low = _low_half_lanes()
    mixed = []
    for c in range(tm // SGU_CHUNK):
        cols = []
        for j in range(SGU_DIM // LANES):
            vj = v[c * SGU_CHUNK:(c + 1) * SGU_CHUNK, j * LANES:(j + 1) * LANES]
            stacked = jnp.concatenate(
                [jnp.where(low, vj, 0.0), jnp.where(low, 0.0, vj)], axis=0).astype(BF16)
            cols.append(_dot(sw_ref[j], stacked))
        mixed.append(jnp.concatenate(cols, axis=1) + sb_ref[...])
    mixed = jnp.concatenate(mixed, axis=0) if len(mixed) > 1 else mixed[0]
    y_sgu = _dot((u * mixed).astype(BF16), sout_ref[...])

    yc = _layer_norm(yc_ref[...], clg_ref[layer:layer + 1, :], clb_ref[layer:layer + 1, :])
    yc = yc * jax.nn.sigmoid(yc)
    y_conv = _dot(yc.astype(BF16), cout_ref[...])
    m = (jax.nn.sigmoid(gates[0]) * y_conv + jax.nn.sigmoid(gates[1]) * y_attn
         + jax.nn.sigmoid(gates[2]) * y_sgu)

    o_ref[0] = x + mod_ref[0, 2:3, :] * _dot(m.astype(BF16), wo_ref[...])


def _merge(x, mod, g, attn, wab, win, gb, cw, cb, clg, clb, cout, aout,
           slg, slb, sw, sb, sout, wo, *, layer, mod_row=None):
    b, l, d = x.shape
    tm = min(TM_MERGE, l)
    hb = tm // CONV_HALO
    nhb = l // CONV_HALO
    weights = (wab, win, gb, cw, cb, clg, clb, cout, aout, slg, slb, sw, sb, sout, wo)
    return pl.pallas_call(
        functools.partial(_merge_kernel, tm=tm, rows=CONV_ROWS, layer=layer),
        grid=(b, l // tm),
        in_specs=[
            pl.BlockSpec((1, tm, d), lambda bi, i: (bi, i, 0)),
            pl.BlockSpec((1, CONV_HALO, d), lambda bi, i: (bi, jnp.maximum(i * hb - 1, 0), 0)),
            pl.BlockSpec((1, CONV_HALO, d),
                         lambda bi, i: (bi, jnp.minimum((i + 1) * hb, nhb - 1), 0)),
            _mod_spec(mod, layer, mod_row),
            _param_spec(g, layer),
            pl.BlockSpec((1, tm, attn.shape[-1]), lambda bi, i: (bi, i, 0)),
        ] + [_param_spec(a, layer) for a in weights],
        out_specs=pl.BlockSpec((1, tm, d), lambda bi, i: (bi, i, 0)),
        out_shape=jax.ShapeDtypeStruct((b, l, d), F32),
        scratch_shapes=[
            pltpu.VMEM((tm + 2 * CONV_HALO, d), BF16),
            pltpu.VMEM((2, SUBLANES, tm + 2 * CONV_HALO, LANES), F32),
            pltpu.VMEM((tm, CONV_DIM), F32),
        ],
        compiler_params=_cparams(2),
        name="mixer_merge",
    )(x, x, x, mod, g, attn, *weights)


def _ffn_kernel(x_ref, xp_ref, xn_ref, mod_ref, g_ref, up_ref, dw_ref, db_ref, down_ref,
                o_ref, h_ref, z_ref, act_ref, *, tm, ffn, rows, layer):
    i = pl.program_id(1)
    n = pl.num_programs(1)
    g = g_ref[layer:layer + 1, :]
    shift = mod_ref[0, 3:4, :]
    scale = mod_ref[0, 4:5, :]
    x = x_ref[0]
    h_ref[...] = jnp.concatenate(
        [jnp.where(i > 0, _norm_mod(xp_ref[0], g, shift, scale), 0.0),
         _norm_mod(x, g, shift, scale),
         jnp.where(i < n - 1, _norm_mod(xn_ref[0], g, shift, scale), 0.0)], axis=0).astype(BF16)
    ck = FFN_CHUNK
    n_chunks = ffn // ck
    ext = rows + 2 * FFN_HALO

    def offsets(c):
        return (c * ck, ffn + c * ck)

    def up_proj(c):
        for half, off in enumerate(offsets(c)):
            z_ref[c % 2, half] = _dot(h_ref[...], up_ref[:, off:off + ck])

    def rows_of(ref, t, off):
        return jnp.concatenate([ref[t * SUBLANES:(t + 1) * SUBLANES, off:off + ck]] * (rows // SUBLANES),
                               axis=0)

    def conv_gate(c):
        for r0 in range(0, tm, rows):
            halves = []
            for half, off in enumerate(offsets(c)):
                z = z_ref[c % 2, half, r0:r0 + ext, :]
                halves.append(
                    rows_of(db_ref, 0, off)
                    + rows_of(dw_ref, 0, off) * pltpu.roll(z, 1, 0)[FFN_HALO:FFN_HALO + rows]
                    + rows_of(dw_ref, 1, off) * z[FFN_HALO:FFN_HALO + rows]
                    + rows_of(dw_ref, 2, off) * pltpu.roll(z, ext - 1, 0)[FFN_HALO:FFN_HALO + rows])
            a, gate = halves
            act_ref[r0:r0 + rows, c * ck:(c + 1) * ck] = (a * jax.nn.sigmoid(a) * gate).astype(BF16)

    up_proj(0)
    for c in range(n_chunks):
        if c + 1 < n_chunks:
            up_proj(c + 1)
        conv_gate(c)
    o_ref[0] = x + mod_ref[0, 5:6, :] * _dot(act_ref[...], down_ref[...])


def _conv_ffn(x, mod, g, up, dw, db, down, *, layer, mod_row=None):
    b, l, d = x.shape
    tm = min(TM_FFN, l)
    ffn = down.shape[1]
    hb = tm // FFN_HALO
    nhb = l // FFN_HALO
    return pl.pallas_call(
        functools.partial(_ffn_kernel, tm=tm, ffn=ffn, rows=min(FFN_ROWS, tm), layer=layer),
        grid=(b, l // tm),
        in_specs=[
            pl.BlockSpec((1, tm, d), lambda bi, i: (bi, i, 0)),
            pl.BlockSpec((1, FFN_HALO, d), lambda bi, i: (bi, jnp.maximum(i * hb - 1, 0), 0)),
            pl.BlockSpec((1, FFN_HALO, d),
                         lambda bi, i: (bi, jnp.minimum((i + 1) * hb, nhb - 1), 0)),
            _mod_spec(mod, layer, mod_row),
            _param_spec(g, layer), _layer_spec(up, layer), _layer_spec(dw, layer), _layer_spec(db, layer),
            _layer_spec(down, layer),
        ],
        out_specs=pl.BlockSpec((1, tm, d), lambda bi, i: (bi, i, 0)),
        out_shape=jax.ShapeDtypeStruct((b, l, d), F32),
        scratch_shapes=[
            pltpu.VMEM((tm + 2 * FFN_HALO, d), BF16),
            pltpu.VMEM((2, 2, tm + 2 * FFN_HALO, FFN_CHUNK), F32),
            pltpu.VMEM((tm, ffn), BF16),
        ],
        compiler_params=_cparams(2),
        name="conv_ffn",
    )(x, x, x, mod, g, up, dw, db, down)


def _rope_tables(seq):
    quarter = HEAD_DIM // 4
    inv = np.power(ROPE_BASE, -np.arange(quarter, dtype=np.float64) / quarter)
    t = np.arange(seq)
    rows = (t // GRID_W)[:, None] * inv[None, :]
    cols = (t % GRID_W)[:, None] * inv[None, :]
    cos = np.concatenate([np.cos(rows)] * 2 + [np.cos(cols)] * 2, axis=1)
    sin = np.concatenate([-np.sin(rows), np.sin(rows), -np.sin(cols), np.sin(cols)], axis=1)
    reps = LANES // HEAD_DIM
    return (np.tile(cos, (1, reps)).astype(np.float32), np.tile(sin, (1, reps)).astype(np.float32))


def kernel(x, c, ctx, c_ctx, ada_w, ada_b, norm1_g, norm2_g, w_in, gate_b, conv_dw_w, conv_dw_b, conv_ln_g, conv_ln_b, conv_out, q_norm_g, k_norm_g, attn_sink, attn_out, sgu_ln_g, sgu_ln_b, sgu_w, sgu_b, sgu_out, w_o, ffn_up, ffn_dw_w, ffn_dw_b, ffn_down):
    bsz, seq, d = x.shape
    depth = ada_w.shape[0]
    n_ctx = ctx.shape[1]

    n_rows = -(-(bsz + 1) // SUBLANES) * SUBLANES
    cond = jnp.concatenate([c, c_ctx[None], jnp.zeros((n_rows - bsz - 1, d), F32)], axis=0)
    mods = _modulation(cond, ada_w, ada_b).reshape(depth, n_rows, 6, d)

    cos, sin = _rope_tables(seq)
    no_rope = np.zeros((n_ctx, LANES), np.float32)
    head_id = np.arange(LANES) // HEAD_DIM
    e = (head_id[:, None] == head_id[None, :]).astype(np.float32)
    e = jnp.asarray(np.concatenate([e, e], axis=0), BF16)

    nab = 2 * CONV_DIM
    n_in1 = nab + (N_HEADS + 2 * N_KV_HEADS) * HEAD_DIM
    groups = CONV_DIM // LANES
    assert w_in.shape[2] - n_in1 == (1 + N_BRANCH) * d and 2 * SGU_DIM == d

    win = w_in.astype(BF16)
    wab = (win[:, :, :nab].reshape(depth, d, 2, groups, LANES)
           .transpose(0, 1, 3, 2, 4).reshape(depth, d, nab))
    gb = gate_b[:, :, None, :]
    gqk = jnp.concatenate([jnp.tile(q_norm_g * (LOG2E * HEAD_DIM ** -0.5), (1, N_HEADS)),
                           jnp.tile(k_norm_g, (1, N_KV_HEADS))], axis=1)
    sink = attn_sink * LOG2E
    sw = sgu_w.reshape(depth, SGU_GROUPS // 2, 2, SGU_CHUNK, SGU_CHUNK)
    sw = jnp.concatenate([sw[:, :, 0], sw[:, :, 1]], axis=-1).astype(BF16)
    sb = jnp.repeat(jnp.swapaxes(sgu_b, 1, 2), SGU_DIM // SGU_GROUPS, axis=2)
    g1 = norm1_g
    g2 = norm2_g
    merge_w = (wab, win, gb, conv_dw_w, conv_dw_b, conv_ln_g, conv_ln_b,
               conv_out.astype(BF16), attn_out.astype(BF16), sgu_ln_g, sgu_ln_b, sw, sb,
               sgu_out.astype(BF16), w_o.astype(BF16))
    ffn_w = (ffn_up.astype(BF16), jnp.repeat(ffn_dw_w, SUBLANES, axis=1),
             jnp.repeat(ffn_dw_b[:, None, :], SUBLANES, axis=1), ffn_down.astype(BF16))

    for l in range(depth):
        last = l == depth - 1
        qc, kc, vc = _in_proj(ctx, mods, g1, win, e, gqk, no_rope, no_rope, rope=False, layer=l,
                              mod_row=bsz)
        qx, kx, vx = _in_proj(x, mods, g1, win, e, gqk, cos, sin, rope=True, layer=l)
        ax = _attention(sink, qx, kc, vc, kx, vx, layer=l)
        x_mid = _merge(x, mods, g1, ax, *merge_w, layer=l)
        x_new = _conv_ffn(x_mid, mods, g2, *ffn_w, layer=l)
        if not last:
            ac = _attention(sink, qc, kc, vc, layer=l)
            c_mid = _merge(ctx, mods, g1, ac, *merge_w, layer=l, mod_row=bsz)
            ctx = _conv_ffn(c_mid, mods, g2, *ffn_w, layer=l, mod_row=bsz)
        x = x_new
    return x
```

```python
import functools

import jax
import jax.numpy as jnp
import numpy as np
from jax import lax
from jax.experimental import pallas as pl
from jax.experimental.pallas import tpu as pltpu

F32 = jnp.float32
BF16 = jnp.bfloat16

EPS = 1e-6
NEG_INF = -1e30
ROPE_BASE = 10000.0
LOG2E = 1.4426950408889634
GRID_W = 64
HEAD_DIM = 64
N_HEADS = 8
N_KV_HEADS = 2
WINDOW = 128
CONV_DIM = 512
CONV_WIDTH = 31
SGU_DIM = 512
SGU_GROUPS = 8
SGU_CHUNK = 128
FFN_CONV_WIDTH = 3
N_BRANCH = 3

LANES = 128
SUBLANES = 8
CONV_HALO = 16
FFN_HALO = 8
FFN_CHUNK = 256
FFN_ROWS = 256
CONV_ROWS = 64
VMEM_LIMIT = 56 * 1024 * 1024
MOD_TN = 1536

TM_IN = 1024
TQ_ATTN = 256
TM_MERGE = 512
TM_FFN = 1024


def _cparams(n_axes):
    return pltpu.CompilerParams(
        dimension_semantics=("parallel",) * n_axes, vmem_limit_bytes=VMEM_LIMIT)


def _dot(a, b):
    return jnp.dot(a, b, preferred_element_type=F32)


def _dot_nt(a, b):
    return lax.dot_general(a, b, (((1,), (1,)), ((), ())), preferred_element_type=F32)


def _norm_mod(x, g, shift, scale):
    ms = jnp.mean(x * x, axis=-1, keepdims=True)
    return x * lax.rsqrt(ms + EPS) * (g * (1.0 + scale)) + shift


def _layer_norm(x, g, b):
    mu = jnp.mean(x, axis=-1, keepdims=True)
    xc = x - mu
    var = jnp.mean(xc * xc, axis=-1, keepdims=True)
    return xc * lax.rsqrt(var + EPS) * g + b


def _low_half_lanes():
    return lax.broadcasted_iota(jnp.int32, (1, LANES), 1) < (LANES // 2)


def _full_spec(a):
    return pl.BlockSpec(a.shape, lambda bi, i: (0,) * a.ndim)


def _mod_spec(mods, layer, row):
    blk = (None, 1) + mods.shape[2:]
    if row is None:
        return pl.BlockSpec(blk, lambda bi, i: (layer, bi, 0, 0))
    return pl.BlockSpec(blk, lambda bi, i: (layer, row, 0, 0))


def _param_spec(a, layer):
    return _full_spec(a) if a.ndim == 2 else _layer_spec(a, layer)


def _layer_spec(a, layer):
    return pl.BlockSpec((None,) + a.shape[1:], lambda bi, i: (layer,) + (0,) * (a.ndim - 1),
                        pipeline_mode=pl.Buffered(1))


def _mod_kernel(c_ref, w_ref, b_ref, o_ref):
    c = c_ref[...]
    s = c * jax.nn.sigmoid(c)
    w = w_ref[0]
    s_hi = s.astype(BF16)
    s_lo = (s - s_hi.astype(F32)).astype(BF16)
    w_hi = w.astype(BF16)
    w_lo = (w - w_hi.astype(F32)).astype(BF16)
    o_ref[0] = _dot(s_hi, w_hi) + _dot(s_lo, w_hi) + _dot(s_hi, w_lo) + b_ref[0]


def _modulation(cond, ada_w, ada_b):
    depth, d, n6 = ada_w.shape
    r = cond.shape[0]
    tn = MOD_TN
    return pl.pallas_call(
        _mod_kernel,
        grid=(depth, n6 // tn),
        in_specs=[
            pl.BlockSpec((r, d), lambda l, j: (0, 0)),
            pl.BlockSpec((1, d, tn), lambda l, j: (l, 0, j)),
            pl.BlockSpec((1, 1, tn), lambda l, j: (l, 0, j)),
        ],
        out_specs=pl.BlockSpec((1, r, tn), lambda l, j: (l, 0, j)),
        out_shape=jax.ShapeDtypeStruct((depth, r, n6), F32),
        compiler_params=_cparams(2),
        name="modulation",
    )(cond, ada_w, ada_b.reshape(depth, 1, n6))


def _in_kernel(x_ref, mod_ref, g_ref, w_ref, e_ref, gqk_ref, cos_ref, sin_ref,
               q_ref, k_ref, v_ref, *, rope, layer):
    x = x_ref[0]
    h = _norm_mod(x, g_ref[layer:layer + 1, :], mod_ref[0, 0:1, :], mod_ref[0, 1:2, :]).astype(BF16)
    qkv = _dot(h, w_ref[:, :(N_HEADS + 2 * N_KV_HEADS) * HEAD_DIM])
    nq = N_HEADS * HEAD_DIM
    nqk = nq + N_KV_HEADS * HEAD_DIM
    e = e_ref[...]
    if rope:
        lane = lax.broadcasted_iota(jnp.int32, (1, LANES), 1)
        first_half = (lane % (HEAD_DIM // 2)) < (HEAD_DIM // 4)
        cos = cos_ref[...]
        sin = sin_ref[...]
    outs = []
    for j in range(nqk // LANES):
        t = qkv[:, j * LANES:(j + 1) * LANES]
        sq = t * t
        hi = sq.astype(BF16)
        lo = (sq - hi.astype(F32)).astype(BF16)
        ss = _dot(jnp.concatenate([hi, lo], axis=1), e)
        tn = t * lax.rsqrt(ss * (1.0 / HEAD_DIM) + EPS) * gqk_ref[layer:layer + 1, j * LANES:(j + 1) * LANES]
        if rope:
            quarter = HEAD_DIM // 4
            partner = jnp.where(first_half,
                                pltpu.roll(tn, LANES - quarter, 1),
                                pltpu.roll(tn, quarter, 1))
            tn = tn * cos + partner * sin
        outs.append(tn)
    q_ref[0] = jnp.concatenate(outs[:nq // LANES], axis=1).astype(BF16)
    low = _low_half_lanes()
    kk = outs[nq // LANES]
    kr = pltpu.roll(kk, LANES // 2, 1)
    k_ref[0] = jnp.concatenate([jnp.where(low, kk, kr), jnp.where(low, kr, kk)], axis=1).astype(BF16)
    vv = qkv[:, nqk:]
    vr = pltpu.roll(vv, LANES // 2, 1)
    v_ref[0] = jnp.concatenate([jnp.where(low, vv, 0.0), jnp.where(low, 0.0, vr),
                                jnp.where(low, vr, 0.0), jnp.where(low, 0.0, vv)], axis=1).astype(BF16)


def _in_proj(x, mod, g, w, e, gqk, cos, sin, *, rope, layer, mod_row=None):
    b, l, d = x.shape
    tm = min(TM_IN, l)
    nq = N_HEADS * HEAD_DIM
    nk = 2 * N_KV_HEADS * HEAD_DIM
    nv = 4 * N_KV_HEADS * HEAD_DIM
    assert (N_HEADS + 2 * N_KV_HEADS) * HEAD_DIM <= 2 * CONV_DIM
    return pl.pallas_call(
        functools.partial(_in_kernel, rope=rope, layer=layer),
        grid=(b, l // tm),
        in_specs=[
            pl.BlockSpec((1, tm, d), lambda bi, i: (bi, i, 0)),
            _mod_spec(mod, layer, mod_row),
            _param_spec(g, layer),
            pl.BlockSpec((None, d, 2 * CONV_DIM), lambda bi, i: (layer, 0, 1),
                         pipeline_mode=pl.Buffered(1)),
            _full_spec(e), _param_spec(gqk, layer),
            pl.BlockSpec((tm, LANES), lambda bi, i: (i, 0)),
            pl.BlockSpec((tm, LANES), lambda bi, i: (i, 0)),
        ],
        out_specs=[
            pl.BlockSpec((1, tm, nq), lambda bi, i: (bi, i, 0)),
            pl.BlockSpec((1, tm, nk), lambda bi, i: (bi, i, 0)),
            pl.BlockSpec((1, tm, nv), lambda bi, i: (bi, i, 0)),
        ],
        out_shape=[
            jax.ShapeDtypeStruct((b, l, nq), BF16),
            jax.ShapeDtypeStruct((b, l, nk), BF16),
            jax.ShapeDtypeStruct((b, l, nv), BF16),
        ],
        compiler_params=_cparams(2),
        name="in_proj_rope" if rope else "in_proj_ctx",
    )(x, mod, g, w, e, gqk, cos, sin)


def _attn_kernel(sink_ref, q_ref, kc_ref, vc_ref, *rest, local, tq, seq, layer):
    if local:
        k_ref, v_ref, bias_ref, o_ref = rest
        i = pl.program_id(1)
        width = tq + 2 * WINDOW
        start = pl.multiple_of(jnp.clip(i * tq - WINDOW, 0, seq - width), WINDOW)
    else:
        (o_ref,) = rest
    low = _low_half_lanes()
    grp = N_HEADS // N_KV_HEADS
    n_ctx = kc_ref.shape[1]

    def scores(h):
        hk = h // grp
        qp = q_ref[0, :, (h // 2) * LANES:(h // 2 + 1) * LANES]
        qh = jnp.where(low, qp, 0.0) if h % 2 == 0 else jnp.where(low, 0.0, qp)
        s = _dot_nt(qh, kc_ref[0, :, hk * LANES:(hk + 1) * LANES])
        if local:
            s_loc = _dot_nt(qh, k_ref[0, pl.ds(start, width), hk * LANES:(hk + 1) * LANES])
            s = jnp.concatenate([s, s_loc + bias_ref[...]], axis=1)
        return s

    def softmax(h, s):
        sink = sink_ref[layer, h]
        m = jnp.maximum(jnp.max(s, axis=-1, keepdims=True), sink)
        p = jnp.exp2(s - m)
        den = jnp.sum(p, axis=-1, keepdims=True) + jnp.exp2(sink - m)
        return p.astype(BF16), 1.0 / den

    def values(h, p, inv):
        vs = slice((2 * (h // grp) + h % 2) * LANES, (2 * (h // grp) + h % 2 + 1) * LANES)
        o = _dot(p[:, :n_ctx], vc_ref[0, :, vs])
        if local:
            o = o + _dot(p[:, n_ctx:], v_ref[0, pl.ds(start, width), vs])
        return o * inv

    s = scores(0)
    o_pair = None
    for h in range(N_HEADS):
        s_next = scores(h + 1) if h + 1 < N_HEADS else None
        p, inv = softmax(h, s)
        o = values(h, p, inv)
        if h % 2 == 0:
            o_pair = o
        else:
            o_ref[0, :, (h // 2) * LANES:(h // 2 + 1) * LANES] = (o_pair + o).astype(o_ref.dtype)
        s = s_next


def _attention(sink, q, kc, vc, k=None, v=None, *, layer):
    b, l, nq = q.shape
    lc = kc.shape[1]
    local = k is not None
    tq = min(TQ_ATTN, l)
    if local:
        assert l >= tq + 2 * WINDOW and tq % WINDOW == 0
    in_specs = [
        pl.BlockSpec(memory_space=pltpu.SMEM),
        pl.BlockSpec((1, tq, nq), lambda bi, i: (bi, i, 0)),
        pl.BlockSpec((1, lc, kc.shape[2]), lambda bi, i: (bi, 0, 0)),
        pl.BlockSpec((1, lc, vc.shape[2]), lambda bi, i: (bi, 0, 0)),
    ]
    args = [sink, q, kc, vc]
    if local:
        width = tq + 2 * WINDOW
        n_off = width // WINDOW - tq // WINDOW + 1
        qpos = np.arange(tq)[None, :, None] + WINDOW * np.arange(n_off)[:, None, None]
        kpos = np.arange(width)[None, None, :]
        bias = np.where(np.abs(qpos - kpos) <= WINDOW, 0.0, NEG_INF).astype(np.float32)

        def bias_index(bi, i):
            start = jnp.clip(i * tq - WINDOW, 0, l - width)
            return ((i * tq - start) // WINDOW, 0, 0)

        in_specs += [pl.BlockSpec((1, l, k.shape[2]), lambda bi, i: (bi, 0, 0)),
                     pl.BlockSpec((1, l, v.shape[2]), lambda bi, i: (bi, 0, 0)),
                     pl.BlockSpec((None, tq, width), bias_index)]
        args += [k, v, bias]
    return pl.pallas_call(
        functools.partial(_attn_kernel, local=local, tq=tq, seq=l, layer=layer),
        grid=(b, l // tq),
        in_specs=in_specs,
        out_specs=pl.BlockSpec((1, tq, nq), lambda bi, i: (bi, i, 0)),
        out_shape=jax.ShapeDtypeStruct((b, l, nq), BF16),
        compiler_params=_cparams(2),
        name="attn_window" if local else "attn_ctx",
    )(*args)


def _merge_kernel(x_ref, xp_ref, xn_ref, mod_ref, g_ref, a_ref,
                  wab_ref, win_ref, gb_ref,
                  cw_ref, cb_ref, clg_ref, clb_ref, cout_ref, aout_ref,
                  slg_ref, slb_ref, sw_ref, sb_ref, sout_ref, wo_ref,
                  o_ref, h_ref, ys_ref, yc_ref, *, tm, rows, layer):
    i = pl.program_id(1)
    n = pl.num_programs(1)
    g = g_ref[layer:layer + 1, :]
    shift = mod_ref[0, 0:1, :]
    scale = mod_ref[0, 1:2, :]
    x = x_ref[0]
    y_attn = _dot(a_ref[0], aout_ref[...])
    h_ref[...] = jnp.concatenate(
        [jnp.where(i > 0, _norm_mod(xp_ref[0], g, shift, scale), 0.0),
         _norm_mod(x, g, shift, scale),
         jnp.where(i < n - 1, _norm_mod(xn_ref[0], g, shift, scale), 0.0)], axis=0).astype(BF16)
    ext = tm + 2 * CONV_HALO
    base = CONV_HALO - CONV_WIDTH // 2
    span = (base + CONV_WIDTH - 1) // SUBLANES * SUBLANES
    n_groups = CONV_DIM // LANES

    def glu_proj(c):
        ab = _dot(h_ref[...], wab_ref[:, 2 * c * LANES:2 * (c + 1) * LANES])
        y = ab[:, :LANES] * jax.nn.sigmoid(ab[:, LANES:])
        ys_ref[c % 2, 0] = y
        for r in range(1, SUBLANES):
            ys_ref[c % 2, r] = pltpu.roll(y, ext - r, 0)

    def taps(c):
        cols = slice(c * LANES, (c + 1) * LANES)
        for r0 in range(0, tm, rows):
            acc = jnp.broadcast_to(cb_ref[layer:layer + 1, cols], (rows, LANES))
            for r in range(SUBLANES):
                slab = ys_ref[c % 2, r, r0:r0 + rows + span, :]
                for t in range(CONV_WIDTH):
                    if (base + t) % SUBLANES == r:
                        lo = base + t - r
                        acc = acc + cw_ref[t:t + 1, cols] * slab[lo:lo + rows]
            yc_ref[r0:r0 + rows, cols] = acc

    def h_main():
        return h_ref[CONV_HALO:CONV_HALO + tm, :]

    d = x.shape[-1]
    rest = 2 * CONV_DIM + (N_HEADS + 2 * N_KV_HEADS) * HEAD_DIM

    def h_proj(k):
        out = _dot(h_main(), win_ref[:, rest + k * d:rest + (k + 1) * d])
        return out if k == 0 else out + gb_ref[k - 1]

    n_pre = 1 + N_BRANCH
    pre = []
    glu_proj(0)
    for c in range(n_groups):
        if c + 1 < n_groups:
            glu_proj(c + 1)
        if len(pre) < n_pre:
            pre.append(h_proj(len(pre)))
        taps(c)
    while len(pre) < n_pre:
        pre.append(h_proj(len(pre)))
    uv, gates = pre[0], pre[1:]

    u = jax.nn.gelu(uv[:, :SGU_DIM])
    v = _layer_norm(jax.nn.gelu(uv[:, SGU_DIM:]), slg_ref[layer:layer + 1, :], slb_ref[layer:layer + 1, :])
    low = _low_half_lanes()
    mixed = []
    for c in range(tm // SGU_CHUNK):
        cols = []
        for j in range(SGU_DIM // LANES):
            vj = v[c * SGU_CHUNK:(c + 1) * SGU_CHUNK, j * LANES:(j + 1) * LANES]
            stacked = jnp.concatenate(
                [jnp.where(low, vj, 0.0), jnp.where(low, 0.0, vj)], axis=0).astype(BF16)
            cols.append(_dot(sw_ref[j], stacked))
        mixed.append(jnp.concatenate(cols, axis=1) + sb_ref[...])
    mixed = jnp.concatenate(mixed, axis=0) if len(mixed) > 1 else mixed[0]
    y_sgu = _dot((u * mixed).astype(BF16), sout_ref[...])

    yc = _layer_norm(yc_ref[...], clg_ref[layer:layer + 1, :], clb_ref[layer:layer + 1, :])
    yc = yc * jax.nn.sigmoid(yc)
    y_conv = _dot(yc.astype(BF16), cout_ref[...])
    m = (jax.nn.sigmoid(gates[0]) * y_conv + jax.nn.sigmoid(gates[1]) * y_attn
         + jax.nn.sigmoid(gates[2]) * y_sgu)

    o_ref[0] = x + mod_ref[0, 2:3, :] * _dot(m.astype(BF16), wo_ref[...])


def _merge(x, mod, g, attn, wab, win, gb, cw, cb, clg, clb, cout, aout,
           slg, slb, sw, sb, sout, wo, *, layer, mod_row=None):
    b, l, d = x.shape
    tm = min(TM_MERGE, l)
    hb = tm // CONV_HALO
    nhb = l // CONV_HALO
    weights = (wab, win, gb, cw, cb, clg, clb, cout, aout, slg, slb, sw, sb, sout, wo)
    return pl.pallas_call(
        functools.partial(_merge_kernel, tm=tm, rows=CONV_ROWS, layer=layer),
        grid=(b, l // tm),
        in_specs=[
            pl.BlockSpec((1, tm, d), lambda bi, i: (bi, i, 0)),
            pl.BlockSpec((1, CONV_HALO, d), lambda bi, i: (bi, jnp.maximum(i * hb - 1, 0), 0)),
            pl.BlockSpec((1, CONV_HALO, d),
                         lambda bi, i: (bi, jnp.minimum((i + 1) * hb, nhb - 1), 0)),
            _mod_spec(mod, layer, mod_row),
            _param_spec(g, layer),
            pl.BlockSpec((1, tm, attn.shape[-1]), lambda bi, i: (bi, i, 0)),
        ] + [_param_spec(a, layer) for a in weights],
        out_specs=pl.BlockSpec((1, tm, d), lambda bi, i: (bi, i, 0)),
        out_shape=jax.ShapeDtypeStruct((b, l, d), F32),
        scratch_shapes=[
            pltpu.VMEM((tm + 2 * CONV_HALO, d), BF16),
            pltpu.VMEM((2, SUBLANES, tm + 2 * CONV_HALO, LANES), F32),
            pltpu.VMEM((tm, CONV_DIM), F32),
        ],
        compiler_params=_cparams(2),
        name="mixer_merge",
    )(x, x, x, mod, g, attn, *weights)


def _ffn_kernel(x_ref, xp_ref, xn_ref, mod_ref, g_ref, up_ref, dw_ref, db_ref, down_ref,
                o_ref, h_ref, z_ref, act_ref, *, tm, ffn, rows, layer):
    i = pl.program_id(1)
    n = pl.num_programs(1)
    g = g_ref[layer:layer + 1, :]
    shift = mod_ref[0, 3:4, :]
    scale = mod_ref[0, 4:5, :]
    x = x_ref[0]
    h_ref[...] = jnp.concatenate(
        [jnp.where(i > 0, _norm_mod(xp_ref[0], g, shift, scale), 0.0),
         _norm_mod(x, g, shift, scale),
         jnp.where(i < n - 1, _norm_mod(xn_ref[0], g, shift, scale), 0.0)], axis=0).astype(BF16)
    ck = FFN_CHUNK
    n_chunks = ffn // ck
    ext = rows + 2 * FFN_HALO

    def offsets(c):
        return (c * ck, ffn + c * ck)

    def up_proj(c):
        for half, off in enumerate(offsets(c)):
            z_ref[c % 2, half] = _dot(h_ref[...], up_ref[:, off:off + ck])

    def rows_of(ref, t, off):
        return jnp.concatenate([ref[t * SUBLANES:(t + 1) * SUBLANES, off:off + ck]] * (rows // SUBLANES),
                               axis=0)

    def conv_gate(c):
        for r0 in range(0, tm, rows):
            halves = []
            for half, off in enumerate(offsets(c)):
                z = z_ref[c % 2, half, r0:r0 + ext, :]
                halves.append(
                    rows_of(db_ref, 0, off)
                    + rows_of(dw_ref, 0, off) * pltpu.roll(z, 1, 0)[FFN_HALO:FFN_HALO + rows]
                    + rows_of(dw_ref, 1, off) * z[FFN_HALO:FFN_HALO + rows]
                    + rows_of(dw_ref, 2, off) * pltpu.roll(z, ext - 1, 0)[FFN_HALO:FFN_HALO + rows])
            a, gate = halves
            act_ref[r0:r0 + rows, c * ck:(c + 1) * ck] = (a * jax.nn.sigmoid(a) * gate).astype(BF16)

    up_proj(0)
    for c in range(n_chunks):
        if c + 1 < n_chunks:
            up_proj(c + 1)
        conv_gate(c)
    o_ref[0] = x + mod_ref[0, 5:6, :] * _dot(act_ref[...], down_ref[...])


def _conv_ffn(x, mod, g, up, dw, db, down, *, layer, mod_row=None):
    b, l, d = x.shape
    tm = min(TM_FFN, l)
    ffn = down.shape[1]
    hb = tm // FFN_HALO
    nhb = l // FFN_HALO
    return pl.pallas_call(
        functools.partial(_ffn_kernel, tm=tm, ffn=ffn, rows=min(FFN_ROWS, tm), layer=layer),
        grid=(b, l // tm),
        in_specs=[
            pl.BlockSpec((1, tm, d), lambda bi, i: (bi, i, 0)),
            pl.BlockSpec((1, FFN_HALO, d), lambda bi, i: (bi, jnp.maximum(i * hb - 1, 0), 0)),
            pl.BlockSpec((1, FFN_HALO, d),
                         lambda bi, i: (bi, jnp.minimum((i + 1) * hb, nhb - 1), 0)),
            _mod_spec(mod, layer, mod_row),
            _param_spec(g, layer), _layer_spec(up, layer), _layer_spec(dw, layer), _layer_spec(db, layer),
            _layer_spec(down, layer),
        ],
        out_specs=pl.BlockSpec((1, tm, d), lambda bi, i: (bi, i, 0)),
        out_shape=jax.ShapeDtypeStruct((b, l, d), F32),
        scratch_shapes=[
            pltpu.VMEM((tm + 2 * FFN_HALO, d), BF16),
            pltpu.VMEM((2, 2, tm + 2 * FFN_HALO, FFN_CHUNK), F32),
            pltpu.VMEM((tm, ffn), BF16),
        ],
        compiler_params=_cparams(2),
        name="conv_ffn",
    )(x, x, x, mod, g, up, dw, db, down)


def _rope_tables(seq):
    quarter = HEAD_DIM // 4
    inv = np.power(ROPE_BASE, -np.arange(quarter, dtype=np.float64) / quarter)
    t = np.arange(seq)
    rows = (t // GRID_W)[:, None] * inv[None, :]
    cols = (t % GRID_W)[:, None] * inv[None, :]
    cos = np.concatenate([np.cos(rows)] * 2 + [np.cos(cols)] * 2, axis=1)
    sin = np.concatenate([-np.sin(rows), np.sin(rows), -np.sin(cols), np.sin(cols)], axis=1)
    reps = LANES // HEAD_DIM
    return (np.tile(cos, (1, reps)).astype(np.float32), np.tile(sin, (1, reps)).astype(np.float32))


def kernel(x, c, ctx, c_ctx, ada_w, ada_b, norm1_g, norm2_g, w_in, gate_b, conv_dw_w, conv_dw_b, conv_ln_g, conv_ln_b, conv_out, q_norm_g, k_norm_g, attn_sink, attn_out, sgu_ln_g, sgu_ln_b, sgu_w, sgu_b, sgu_out, w_o, ffn_up, ffn_dw_w, ffn_dw_b, ffn_down):
    bsz, seq, d = x.shape
    depth = ada_w.shape[0]
    n_ctx = ctx.shape[1]

    n_rows = -(-(bsz + 1) // SUBLANES) * SUBLANES
    cond = jnp.concatenate([c, c_ctx[None], jnp.zeros((n_rows - bsz - 1, d), F32)], axis=0)
    mods = _modulation(cond, ada_w, ada_b).reshape(depth, n_rows, 6, d)

    cos, sin = _rope_tables(seq)
    no_rope = np.zeros((n_ctx, LANES), np.float32)
    head_id = np.arange(LANES) // HEAD_DIM
    e = (head_id[:, None] == head_id[None, :]).astype(np.float32)
    e = jnp.asarray(np.concatenate([e, e], axis=0), BF16)

    nab = 2 * CONV_DIM
    n_in1 = nab + (N_HEADS + 2 * N_KV_HEADS) * HEAD_DIM
    groups = CONV_DIM // LANES
    assert w_in.shape[2] - n_in1 == (1 + N_BRANCH) * d and 2 * SGU_DIM == d

    win = w_in.astype(BF16)
    wab = (win[:, :, :nab].reshape(depth, d, 2, groups, LANES)
           .transpose(0, 1, 3, 2, 4).reshape(depth, d, nab))
    gb = gate_b[:, :, None, :]
    gqk = jnp.concatenate([jnp.tile(q_norm_g * (LOG2E * HEAD_DIM ** -0.5), (1, N_HEADS)),
                           jnp.tile(k_norm_g, (1, N_KV_HEADS))], axis=1)
    sink = attn_sink * LOG2E
    sw = sgu_w.reshape(depth, SGU_GROUPS // 2, 2, SGU_CHUNK, SGU_CHUNK)
    sw = jnp.concatenate([sw[:, :, 0], sw[:, :, 1]], axis=-1).astype(BF16)
    sb = jnp.repeat(jnp.swapaxes(sgu_b, 1, 2), SGU_DIM // SGU_GROUPS, axis=2)
    g1 = norm1_g
    g2 = norm2_g
    merge_w = (wab, win, gb, conv_dw_w, conv_dw_b, conv_ln_g, conv_ln_b,
               conv_out.astype(BF16), attn_out.astype(BF16), sgu_ln_g, sgu_ln_b, sw, sb,
               sgu_out.astype(BF16), w_o.astype(BF16))
    ffn_w = (ffn_up.astype(BF16), jnp.repeat(ffn_dw_w, SUBLANES, axis=1),
             jnp.repeat(ffn_dw_b[:, None, :], SUBLANES, axis=1), ffn_down.astype(BF16))

    for l in range(depth):
        last = l == depth - 1
        qc, kc, vc = _in_proj(ctx, mods, g1, win, e, gqk, no_rope, no_rope, rope=False, layer=l,
                              mod_row=bsz)
        qx, kx, vx = _in_proj(x, mods, g1, win, e, gqk, cos, sin, rope=True, layer=l)
        ax = _attention(sink, qx, kc, vc, kx, vx, layer=l)
        x_mid = _merge(x, mods, g1, ax, *merge_w, layer=l)
        x_new = _conv_ffn(x_mid, mods, g2, *ffn_w, layer=l)
        if not last:
            ac = _attention(sink, qc, kc, vc, layer=l)
            c_mid = _merge(ctx, mods, g1, ac, *merge_w, layer=l, mod_row=bsz)
            ctx = _conv_ffn(c_mid, mods, g2, *ffn_w, layer=l, mod_row=bsz)
        x = x_new
    return x
```
